```python
import math
import jax
import jax.numpy as jnp
from jax import lax
import numpy as np

D_MODEL = 4096
BATCH = 2
SEQ = 8192
DEPTH = 4

MIX_WIDTH = D_MODEL
DN_HEADS = 16
DN_HEAD_DIM = 128
DN_WIDTH = DN_HEADS * DN_HEAD_DIM
DN_CHUNK = 64
CONV_WIDTH = 4
POOL_WINDOWS = (2, 4, 8, 16)
POOL_WIDTH = MIX_WIDTH - DN_WIDTH
POOL_GROUP = POOL_WIDTH // len(POOL_WINDOWS)
SG_WIDTH = MIX_WIDTH
SG_HEADS = 16
SG_HEAD_DIM = SG_WIDTH // SG_HEADS
SG_CHUNK = 128
N_EXPERTS = 16
N_GROUPS = 4
EXPERTS_PER_GROUP = N_EXPERTS // N_GROUPS
TOP_K = 2
D_EXPERT = 512
N_EVEN = (DEPTH + 1) // 2
N_ODD = DEPTH // 2
EV_IN = 4 * DN_WIDTH + 2 * DN_HEADS + POOL_WIDTH
DEEPNORM_ALPHA = (2.0 * DEPTH) ** 0.25
DEEPNORM_BETA = (8.0 * DEPTH) ** -0.25
LN_EPS = 1e-5
RMS_EPS = 1e-6

kernel_name = 'hybrid_deltanet_pool_sgu_grouped_moe'


def layer_norm(x, g, b):
    xf = x.astype(jnp.float32)
    mu = jnp.mean(xf, axis=-1, keepdims=True)
    xc = xf - mu
    var = jnp.mean(xc * xc, axis=-1, keepdims=True)
    return (xc * lax.rsqrt(var + LN_EPS) * g + b).astype(x.dtype)


def l2_normalize(x):
    xf = x.astype(jnp.float32)
    return xf * lax.rsqrt(jnp.sum(xf * xf, axis=-1, keepdims=True) + RMS_EPS)


def causal_short_conv(x, w):
    T = x.shape[1]
    xp = jnp.pad(x, ((0, 0), (CONV_WIDTH - 1, 0), (0, 0)))
    y = xp[:, 0:T] * w[0]
    for j in range(1, CONV_WIDTH):
        y = y + xp[:, j:j + T] * w[j]
    return jax.nn.silu(y)


def gated_delta_rule(q, k, v, g, beta):
    Bn, T, H, Dk = q.shape
    Dv = v.shape[-1]
    C = DN_CHUNK
    N = T // C

    def chunks(t):
        t = t.astype(jnp.float32).reshape((Bn, N, C, H) + t.shape[3:])
        return jnp.moveaxis(t, (1, 3), (0, 2))

    q, k, v, g, beta = chunks(q), chunks(k), chunks(v), chunks(g), chunks(beta)
    gc = jnp.cumsum(g, axis=-1)
    causal = jnp.tril(jnp.ones((C, C), dtype=bool))
    strict = jnp.tril(jnp.ones((C, C), dtype=bool), -1)
    diff = gc[..., :, None] - gc[..., None, :]
    decay = jnp.where(causal, jnp.exp(jnp.where(causal, diff, 0.0)), 0.0)
    kb = k * beta[..., None]
    a_strict = jnp.where(strict, jnp.einsum('nbhck,nbhsk->nbhcs', kb, k) * decay, 0.0)
    eye = jnp.eye(C, dtype=jnp.float32)
    rhs = jnp.concatenate([v * beta[..., None], kb * jnp.exp(gc)[..., None]], axis=-1)
    sol = lax.linalg.triangular_solve(a_strict + eye, rhs, left_side=True, lower=True,
                                      unit_diagonal=True)
    u, w = sol[..., :Dv], sol[..., Dv:]
    qk = jnp.einsum('nbhck,nbhsk->nbhcs', q, k) * decay
    g_last = gc[..., -1]
    q_dec = q * jnp.exp(gc)[..., None]
    k_dec = k * jnp.exp(g_last[..., None] - gc)[..., None]

    def step(S, xs):
        qd, kd, u_i, w_i, qk_i, gl = xs
        v_new = u_i - jnp.einsum('bhck,bhkv->bhcv', w_i, S)
        o = jnp.einsum('bhck,bhkv->bhcv', qd, S) + jnp.einsum('bhcs,bhsv->bhcv', qk_i, v_new)
        S = S * jnp.exp(gl)[..., None, None] + jnp.einsum('bhck,bhcv->bhkv', kd, v_new)
        return S, o

    S0 = jnp.zeros((Bn, H, Dk, Dv), dtype=jnp.float32)
    _, o = lax.scan(step, S0, (q_dec, k_dec, u, w, qk, g_last))
    return jnp.moveaxis(o, (0, 2), (1, 3)).reshape(Bn, T, H, Dv)


def multiscale_pool(xp, pool_w, pool_b, pool_scale):
    Bn, T, _ = xp.shape
    xf = xp.astype(jnp.float32).reshape(Bn, T, len(POOL_WINDOWS), POOL_GROUP)
    csum = jnp.cumsum(xf, axis=1)
    pos = jnp.arange(T)
    groups = []
    for gi, win in enumerate(POOL_WINDOWS):
        c = csum[:, :, gi]
        lagged = jnp.pad(c, ((0, 0), (win, 0), (0, 0)))[:, :T]
        count = jnp.minimum(pos + 1, win).astype(jnp.float32)[None, :, None]
        groups.append((c - lagged) / count - xf[:, :, gi])
    pooled = jnp.stack(groups, axis=2)
    mixed = jnp.einsum('btgc,gce->btge', pooled, pool_w.astype(jnp.float32)) + pool_b
    return (mixed.reshape(Bn, T, POOL_WIDTH) * pool_scale).astype(xp.dtype)


def even_mixer(x, w_in, w_conv, a_log, dt_bias, norm_g, pool_w, pool_b, pool_scale, w_out):
    Bn, T, _ = x.shape
    proj = jnp.einsum('btd,de->bte', x, w_in)
    o0 = 4 * DN_WIDTH
    qkv = causal_short_conv(proj[..., :3 * DN_WIDTH], w_conv)
    z = proj[..., 3 * DN_WIDTH:o0]
    a = proj[..., o0:o0 + DN_HEADS]
    b = proj[..., o0 + DN_HEADS:o0 + 2 * DN_HEADS]
    xp = proj[..., o0 + 2 * DN_HEADS:]
    q, k, v = jnp.split(qkv, 3, axis=-1)
    hs = (Bn, T, DN_HEADS, DN_HEAD_DIM)
    q = l2_normalize(q.reshape(hs)) * (DN_HEAD_DIM ** -0.5)
    k = l2_normalize(k.reshape(hs))
    v = v.reshape(hs)
    g = -jnp.exp(a_log.astype(jnp.float32)) * jax.nn.softplus(
        a.astype(jnp.float32) + dt_bias.astype(jnp.float32))
    beta = jax.nn.sigmoid(b.astype(jnp.float32))
    o = gated_delta_rule(q, k, v, g, beta)
    o = o * lax.rsqrt(jnp.mean(o * o, axis=-1, keepdims=True) + RMS_EPS) * norm_g
    o = o * jax.nn.silu(z.reshape(hs).astype(jnp.float32))
    y_a = o.reshape(Bn, T, DN_WIDTH).astype(x.dtype)
    y_b = multiscale_pool(xp, pool_w, pool_b, pool_scale)
    y = jnp.concatenate([y_a, y_b], axis=-1)
    return jnp.einsum('bte,ed->btd', y, w_out)


def odd_mixer(x, w_in, b_in, ln_g, ln_b, w_s, b_s, w_out):
    Bn, T, _ = x.shape
    h = jax.nn.gelu(jnp.einsum('btd,de->bte', x, w_in) + b_in)
    u, v = h[..., :SG_WIDTH], h[..., SG_WIDTH:]
    v = layer_norm(v, ln_g, ln_b)
    vc = v.reshape(Bn, T // SG_CHUNK, SG_CHUNK, SG_HEADS, SG_HEAD_DIM)
    causal = jnp.tril(jnp.ones((SG_CHUNK, SG_CHUNK), dtype=bool))
    ws = jnp.where(causal, w_s, 0.0)
    mixed = jnp.einsum('gts,bnsgc->bntgc', ws, vc) + jnp.transpose(b_s)[None, None, :, :, None]
    gated = u * mixed.reshape(Bn, T, SG_WIDTH).astype(u.dtype)
    return jnp.einsum('bte,ed->btd', gated, w_out)


def grouped_moe(x, router_w, router_bias, w_gate, w_up, w_down):
    Bn, T, D = x.shape
    h = x.reshape(Bn * T, D)
    probs = jax.nn.softmax(jnp.einsum('nd,de->ne', h, router_w).astype(jnp.float32), axis=-1)
    sel = probs + router_bias.astype(jnp.float32)
    grouped = sel.reshape(-1, N_GROUPS, EXPERTS_PER_GROUP)
    group_score = jnp.sum(lax.top_k(grouped, TOP_K)[0], axis=-1)
    best = jnp.argmax(group_score, axis=-1)
    in_group = jnp.take_along_axis(grouped, best[:, None, None], axis=1)[:, 0]
    _, local = lax.top_k(in_group, TOP_K)
    idx = best[:, None] * EXPERTS_PER_GROUP + local
    gate = jnp.take_along_axis(probs, idx, axis=-1)
    gate = gate / jnp.sum(gate, axis=-1, keepdims=True)
    combine = jnp.sum(jax.nn.one_hot(idx, N_EXPERTS, dtype=jnp.float32) * gate[..., None], axis=1)
    act = jax.nn.silu(jnp.einsum('nd,edf->nef', h, w_gate)) * jnp.einsum('nd,edf->nef', h, w_up)
    act = act * combine[..., None].astype(act.dtype)
    y = jnp.einsum('nef,efd->nd', act, w_down)
    return y.reshape(Bn, T, D)


def setup_inputs(seed: int = 0) -> dict:
    key = jax.random.key(seed)
    ks = jax.random.split(key, 24)
    f32 = jnp.float32
    nrm = lambda k, s: jax.random.normal(k, s, dtype=f32)
    dt = jnp.exp(jax.random.uniform(ks[4], (N_EVEN, DN_HEADS), dtype=f32,
                                    minval=math.log(1e-3), maxval=math.log(1e-1)))
    return {
        'x': nrm(ks[0], (BATCH, SEQ, D_MODEL)),
        'ev_w_in': nrm(ks[1], (N_EVEN, D_MODEL, EV_IN)) * D_MODEL ** -0.5,
        'ev_conv': nrm(ks[2], (N_EVEN, CONV_WIDTH, 3 * DN_WIDTH)) * CONV_WIDTH ** -0.5,
        'ev_a_log': jnp.log(jax.random.uniform(ks[3], (N_EVEN, DN_HEADS), dtype=f32, minval=1.0, maxval=16.0)),
        'ev_dt_bias': dt + jnp.log(-jnp.expm1(-dt)),
        'ev_norm_g': 1.0 + 0.02 * nrm(ks[5], (N_EVEN, DN_HEAD_DIM)),
        'ev_pool_w': nrm(ks[6], (N_EVEN, len(POOL_WINDOWS), POOL_GROUP, POOL_GROUP)) * POOL_GROUP ** -0.5,
        'ev_pool_b': 0.02 * nrm(ks[7], (N_EVEN, len(POOL_WINDOWS), POOL_GROUP)),
        'ev_pool_scale': 1.0 + 0.02 * nrm(ks[8], (N_EVEN, POOL_WIDTH)),
        'ev_w_out': nrm(ks[9], (N_EVEN, MIX_WIDTH, D_MODEL)) * (MIX_WIDTH ** -0.5 * DEEPNORM_BETA),
        'od_w_in': nrm(ks[10], (N_ODD, D_MODEL, 2 * SG_WIDTH)) * D_MODEL ** -0.5,
        'od_b_in': 0.02 * nrm(ks[11], (N_ODD, 2 * SG_WIDTH)),
        'od_ln_g': 1.0 + 0.02 * nrm(ks[12], (N_ODD, SG_WIDTH)),
        'od_ln_b': 0.02 * nrm(ks[13], (N_ODD, SG_WIDTH)),
        'od_w_s': nrm(ks[14], (N_ODD, SG_HEADS, SG_CHUNK, SG_CHUNK)) * SG_CHUNK ** -0.5,
        'od_b_s': 1.0 + 0.02 * nrm(ks[15], (N_ODD, SG_HEADS, SG_CHUNK)),
        'od_w_out': nrm(ks[16], (N_ODD, SG_WIDTH, D_MODEL)) * (SG_WIDTH ** -0.5 * DEEPNORM_BETA),
        'ln_g': 1.0 + 0.02 * nrm(ks[17], (DEPTH, 2, D_MODEL)),
        'ln_b': 0.02 * nrm(ks[18], (DEPTH, 2, D_MODEL)),
        'router_w': nrm(ks[19], (D_MODEL, N_EXPERTS)) * D_MODEL ** -0.5,
        'router_bias': 0.01 * nrm(ks[20], (N_EXPERTS,)),
        'moe_w_gate': nrm(ks[21], (DEPTH, N_EXPERTS, D_MODEL, D_EXPERT)) * D_MODEL ** -0.5,
        'moe_w_up': nrm(ks[22], (DEPTH, N_EXPERTS, D_MODEL, D_EXPERT)) * D_MODEL ** -0.5,
        'moe_w_down': nrm(ks[23], (DEPTH, N_EXPERTS, D_EXPERT, D_MODEL)) * (D_EXPERT ** -0.5 * DEEPNORM_BETA),
    }


def reference(x, ev_w_in, ev_conv, ev_a_log, ev_dt_bias, ev_norm_g, ev_pool_w, ev_pool_b,
              ev_pool_scale, ev_w_out, od_w_in, od_b_in, od_ln_g, od_ln_b, od_w_s, od_b_s,
              od_w_out, ln_g, ln_b, router_w, router_bias, moe_w_gate, moe_w_up, moe_w_down):
    h = x
    for layer in range(DEPTH):
        i = layer // 2
        if layer % 2 == 0:
            mix = even_mixer(h, ev_w_in[i], ev_conv[i], ev_a_log[i], ev_dt_bias[i], ev_norm_g[i],
                             ev_pool_w[i], ev_pool_b[i], ev_pool_scale[i], ev_w_out[i])
        else:
            mix = odd_mixer(h, od_w_in[i], od_b_in[i], od_ln_g[i], od_ln_b[i], od_w_s[i],
                            od_b_s[i], od_w_out[i])
        h = layer_norm(DEEPNORM_ALPHA * h + mix, ln_g[layer, 0], ln_b[layer, 0])
        ffn = grouped_moe(h, router_w, router_bias, moe_w_gate[layer], moe_w_up[layer],
                          moe_w_down[layer])
        h = layer_norm(DEEPNORM_ALPHA * h + ffn, ln_g[layer, 1], ln_b[layer, 1])
    return h
```

```python
import functools

import jax
import jax.numpy as jnp
from jax import lax
from jax.experimental import pallas as pl
from jax.experimental.pallas import tpu as pltpu

F32 = jnp.float32
BF16 = jnp.bfloat16
HIGHEST = lax.Precision.HIGHEST

V7X_LANES = 128
V7X_SUBLANES = 8
V7X_VMEM_BYTES = 64 * 1024 * 1024
VMEM_LIMIT = V7X_VMEM_BYTES - 8 * 1024 * 1024

DN_CHUNK = 64
POOL_WINDOWS = (2, 4, 8, 16)
N_GROUPS = 4
TOP_K = 2
LN_EPS = 1e-5
RMS_EPS = 1e-6

MM_TM = 1024
MM_TN = 1024
TOK_TILE = 256
EXP_TILE = 256
DN_TB = 256
DN_HB = 2
AB_TILE = 256
POOL_TILE = 256
SGU_TILE = 256


def _cparams(sem):
    return pltpu.CompilerParams(dimension_semantics=sem, vmem_limit_bytes=VMEM_LIMIT)


def _nt(a, b, precision=None):
    return lax.dot_general(a, b, (((1,), (1,)), ((), ())), precision=precision,
                           preferred_element_type=F32)


def _tn(a, b, precision=None):
    return lax.dot_general(a, b, (((0,), (0,)), ((), ())), precision=precision,
                           preferred_element_type=F32)


def _dot(a, b, precision=None):
    return jnp.dot(a, b, precision=precision, preferred_element_type=F32)


def _mm_kernel(*refs, act, has_bias):
    if has_bias:
        x_ref, w_ref, b_ref, o_ref = refs
    else:
        x_ref, w_ref, o_ref = refs
    acc = _dot(x_ref[...], w_ref[...])
    if has_bias:
        acc = acc + b_ref[...]
    if act == "gelu":
        acc = jax.nn.gelu(acc)
    o_ref[...] = acc.astype(o_ref.dtype)


def matmul(x, w, bias=None, act=None, out_dtype=BF16):
    m, k = x.shape
    n = w.shape[1]
    tm, tn = min(MM_TM, m), min(MM_TN, n)
    assert m % tm == 0 and n % tn == 0
    in_specs = [pl.BlockSpec((tm, k), lambda i, j: (i, 0)),
                pl.BlockSpec((k, tn), lambda i, j: (0, j))]
    args = [x, w]
    if bias is not None:
        in_specs.append(pl.BlockSpec((1, tn), lambda i, j: (0, j)))
        args.append(bias.reshape(1, n).astype(F32))
    return pl.pallas_call(
        functools.partial(_mm_kernel, act=act, has_bias=bias is not None),
        grid=(m // tm, n // tn),
        in_specs=in_specs,
        out_specs=pl.BlockSpec((tm, tn), lambda i, j: (i, j)),
        out_shape=jax.ShapeDtypeStruct((m, n), out_dtype),
        compiler_params=_cparams(("parallel", "arbitrary")),
        name="matmul",
    )(*args)


def _layer_norm_rows(x, g, b):
    mu = jnp.mean(x, axis=-1, keepdims=True)
    xc = x - mu
    var = jnp.mean(xc * xc, axis=-1, keepdims=True)
    return xc * lax.rsqrt(var + LN_EPS) * g + b


def _ln_router_kernel(h_ref, mix_ref, g_ref, b_ref, rwt_ref, rb_ref,
                      h1_ref, dest_ref, gate_ref, cnt_ref, carry_ref,
                      *, alpha, n_experts, cap):
    i = pl.program_id(0)
    epg = n_experts // N_GROUPS
    tm = h_ref.shape[0]

    @pl.when(i == 0)
    def _():
        carry_ref[...] = jnp.zeros_like(carry_ref)

    x = alpha * h_ref[...] + mix_ref[...].astype(F32)
    y = _layer_norm_rows(x, g_ref[...], b_ref[...])
    h1_ref[...] = y

    logits = _nt(rwt_ref[...], y, precision=HIGHEST)
    mx = jnp.max(logits, axis=0, keepdims=True)
    ex = jnp.exp(logits - mx)
    probs = ex / jnp.sum(ex, axis=0, keepdims=True)
    sel = probs + rb_ref[...]

    srow = [sel[e:e + 1, :] for e in range(n_experts)]
    prow = [probs[e:e + 1, :] for e in range(n_experts)]

    def top2_sum(vals):
        best = None
        for a in range(len(vals)):
            for c in range(a + 1, len(vals)):
                s = vals[a] + vals[c]
                best = s if best is None else jnp.maximum(best, s)
        return best

    gscore = [top2_sum(srow[g * epg:(g + 1) * epg]) for g in range(N_GROUPS)]
    best_g = jnp.zeros_like(gscore[0], dtype=jnp.int32)
    best_s = gscore[0]
    for g in range(1, N_GROUPS):
        upd = gscore[g] > best_s
        best_g = jnp.where(upd, g, best_g)
        best_s = jnp.where(upd, gscore[g], best_s)

    def pick(rows, j):
        out = rows[j]
        for g in range(1, N_GROUPS):
            out = jnp.where(best_g == g, rows[g * epg + j], out)
        return out

    in_s = [pick(srow, j) for j in range(epg)]
    in_p = [pick(prow, j) for j in range(epg)]

    def argmax_first(vals, exclude=None):
        bi = None
        bv = None
        for j, v in enumerate(vals):
            if exclude is not None:
                v = jnp.where(exclude == j, -jnp.inf, v)
            if bi is None:
                bi = jnp.zeros_like(best_g)
                bv = v
            else:
                upd = v > bv
                bi = jnp.where(upd, j, bi)
                bv = jnp.where(upd, v, bv)
        return bi

    loc0 = argmax_first(in_s)
    loc1 = argmax_first(in_s, exclude=loc0)

    def take(vals, idx):
        out = vals[0]
        for j in range(1, len(vals)):
            out = jnp.where(idx == j, vals[j], out)
        return out

    p0 = take(in_p, loc0)
    p1 = take(in_p, loc1)
    psum = p0 + p1
    gate_ref[...] = jnp.concatenate([p0 / psum, p1 / psum], axis=0)[None]

    idx0 = best_g * epg + loc0
    idx1 = best_g * epg + loc1

    eid = lax.broadcasted_iota(jnp.int32, (n_experts, tm), 0)
    hit0 = eid == idx0
    hit1 = eid == idx1
    member = jnp.where(hit0, 1.0, jnp.where(hit1, 1.0, 0.0))
    srci = lax.broadcasted_iota(jnp.int32, (tm, tm), 0)
    dsti = lax.broadcasted_iota(jnp.int32, (tm, tm), 1)
    upper = jnp.where(srci < dsti, 1.0, 0.0).astype(BF16)
    prefix = _dot(member.astype(BF16), upper)
    carry = carry_ref[...]
    pos = eid.astype(F32) * float(cap) + carry[:, :1] + prefix
    dest0 = jnp.sum(jnp.where(hit0, pos, 0.0), axis=0, keepdims=True)
    dest1 = jnp.sum(jnp.where(hit1, pos, 0.0), axis=0, keepdims=True)
    dest_ref[...] = jnp.concatenate([dest0, dest1], axis=0).astype(jnp.int32)[None]
    carry = carry + jnp.sum(member, axis=1, keepdims=True)
    carry_ref[...] = carry
    cnt_ref[...] = carry.astype(jnp.int32)


def ln_router(h, mix, ln_g, ln_b, router_wt, router_bias, *, alpha, cap):
    n, d = h.shape
    n_experts = router_wt.shape[0]
    tm = min(TOK_TILE, n)
    nb = n // tm
    row = lambda i: (i, 0)
    const = lambda i: (0, 0)
    return pl.pallas_call(
        functools.partial(_ln_router_kernel, alpha=alpha, n_experts=n_experts, cap=cap),
        grid=(nb,),
        in_specs=[pl.BlockSpec((tm, d), row), pl.BlockSpec((tm, d), row),
                  pl.BlockSpec((1, d), const), pl.BlockSpec((1, d), const),
                  pl.BlockSpec((n_experts, d), const), pl.BlockSpec((n_experts, 1), const)],
        out_specs=[pl.BlockSpec((tm, d), row),
                   pl.BlockSpec((1, TOP_K, tm), lambda i: (i, 0, 0)),
                   pl.BlockSpec((1, TOP_K, tm), lambda i: (i, 0, 0)),
                   pl.BlockSpec((n_experts, V7X_LANES), const)],
        out_shape=[jax.ShapeDtypeStruct((n, d), F32),
                   jax.ShapeDtypeStruct((nb, TOP_K, tm), jnp.int32),
                   jax.ShapeDtypeStruct((nb, TOP_K, tm), F32),
                   jax.ShapeDtypeStruct((n_experts, V7X_LANES), jnp.int32)],
        scratch_shapes=[pltpu.VMEM((n_experts, V7X_LANES), F32)],
        compiler_params=_cparams(("arbitrary",)),
        name="ln_router",
    )(h, mix, ln_g.reshape(1, d), ln_b.reshape(1, d), router_wt, router_bias.reshape(n_experts, 1))


def _row_copy(src_ref, src_row, dst_ref, dst_row, sem):
    return pltpu.make_async_copy(src_ref.at[pl.ds(src_row, 1), :], dst_ref.at[pl.ds(dst_row, 1), :], sem)


def _dispatch_kernel(cnt_ref, dest_ref, x_ref, xs_ref, zero_ref, sem, zsem, *, cap, n_experts):
    i = pl.program_id(0)
    tm = x_ref.shape[0]

    def start(r, c):
        for k in range(TOP_K):
            _row_copy(x_ref, r, xs_ref, dest_ref[k, r], sem).start()
        return c

    lax.fori_loop(0, tm, start, 0)

    @pl.when(i == pl.num_programs(0) - 1)
    def _():
        zero_ref[...] = jnp.zeros_like(zero_ref)

        def pad_rows(e, do):
            cnt = cnt_ref[e]
            n_pad = lax.rem(EXP_TILE - lax.rem(cnt, EXP_TILE), EXP_TILE)

            def one(r, c):
                do(_row_copy(zero_ref, 0, xs_ref, e * cap + cnt + r, zsem))
                return c

            lax.fori_loop(0, n_pad, one, 0)
            return do

        lax.fori_loop(0, n_experts, lambda e, c: (pad_rows(e, lambda cp: cp.start()), c)[1], 0)
        lax.fori_loop(0, n_experts, lambda e, c: (pad_rows(e, lambda cp: cp.wait()), c)[1], 0)

    def wait(r, c):
        for k in range(TOP_K):
            _row_copy(x_ref, 0, xs_ref, 0, sem).wait()
        return c

    lax.fori_loop(0, tm, wait, 0)


def dispatch(x, dest, counts, *, cap):
    n, d = x.shape
    n_experts = counts.shape[0]
    nb, _, tm = dest.shape
    return pl.pallas_call(
        functools.partial(_dispatch_kernel, cap=cap, n_experts=n_experts),
        grid_spec=pltpu.PrefetchScalarGridSpec(
            num_scalar_prefetch=1,
            grid=(nb,),
            in_specs=[pl.BlockSpec((None, TOP_K, tm), lambda i, c: (i, 0, 0), memory_space=pltpu.SMEM),
                      pl.BlockSpec((tm, d), lambda i, c: (i, 0))],
            out_specs=pl.BlockSpec(memory_space=pl.ANY),
            scratch_shapes=[pltpu.VMEM((V7X_SUBLANES, d), x.dtype),
                            pltpu.SemaphoreType.DMA(()), pltpu.SemaphoreType.DMA(())],
        ),
        out_shape=jax.ShapeDtypeStruct((n_experts * cap, d), x.dtype),
        compiler_params=_cparams(("arbitrary",)),
        name="dispatch",
    )(counts, dest, x)


def _expert_kernel(blk_ref, exp_ref, valid_ref, x_ref, wg_ref, wu_ref, wd_ref, o_ref):
    s = pl.program_id(0)

    @pl.when(valid_ref[s] == 1)
    def _():
        x = x_ref[...].astype(BF16)
        gate = _dot(x, wg_ref[...])
        up = _dot(x, wu_ref[...])
        act = (jax.nn.silu(gate) * up).astype(BF16)
        o_ref[...] = _dot(act, wd_ref[...]).astype(o_ref.dtype)


def expert_ffn(xs, w_gate, w_up, w_down, layer, counts, *, cap, n_tokens):
    d = xs.shape[1]
    n_experts, f = w_gate.shape[1], w_gate.shape[3]
    tm = EXP_TILE
    n_tiles = (TOP_K * n_tokens) // tm + n_experts
    tiles_e = (counts + tm - 1) // tm
    cum = jnp.cumsum(tiles_e)
    total = cum[-1]
    slot = jnp.minimum(jnp.arange(n_tiles, dtype=jnp.int32), total - 1)
    e_of = jnp.sum((slot[:, None] >= cum[None, :]).astype(jnp.int32), axis=1)
    first = jnp.take(cum - tiles_e, e_of)
    blk = (e_of * (cap // tm) + slot - first).astype(jnp.int32)
    valid = (jnp.arange(n_tiles) < total).astype(jnp.int32)
    xmap = lambda s, blk, ex, va: (blk[s], 0)
    return pl.pallas_call(
        _expert_kernel,
        grid_spec=pltpu.PrefetchScalarGridSpec(
            num_scalar_prefetch=3,
            grid=(n_tiles,),
            in_specs=[pl.BlockSpec((tm, d), xmap),
                      pl.BlockSpec((None, None, d, f), lambda s, blk, ex, va: (layer, ex[s], 0, 0)),
                      pl.BlockSpec((None, None, d, f), lambda s, blk, ex, va: (layer, ex[s], 0, 0)),
                      pl.BlockSpec((None, None, f, d), lambda s, blk, ex, va: (layer, ex[s], 0, 0))],
            out_specs=pl.BlockSpec((tm, d), xmap),
        ),
        out_shape=jax.ShapeDtypeStruct(xs.shape, F32),
        compiler_params=_cparams(("arbitrary",)),
        name="expert_ffn",
    )(blk, e_of.astype(jnp.int32), valid, xs, w_gate, w_up, w_down)


def _combine_kernel(dest_ref, h_ref, gate_ref, g_ref, b_ref, ys_ref, o_ref, ob_ref, buf_ref, sem, *, alpha):
    tm = h_ref.shape[0]

    def start(r, c):
        for k in range(TOP_K):
            _row_copy(ys_ref, dest_ref[k, r], buf_ref.at[k], r, sem).start()
        return c

    lax.fori_loop(0, tm, start, 0)

    def wait(r, c):
        for k in range(TOP_K):
            _row_copy(ys_ref, 0, buf_ref.at[k], 0, sem).wait()
        return c

    lax.fori_loop(0, tm, wait, 0)

    gates = gate_ref[...]
    ffn = gates[:, 0:1] * buf_ref[0] + gates[:, 1:2] * buf_ref[1]
    x = alpha * h_ref[...] + ffn
    y = _layer_norm_rows(x, g_ref[...], b_ref[...])
    o_ref[...] = y
    ob_ref[...] = y.astype(BF16)


def combine(h, dest, gates_cols, ln_g, ln_b, ys, *, alpha):
    n, d = h.shape
    nb, _, tm = dest.shape
    row = lambda i: (i, 0)
    const = lambda i: (0, 0)
    return pl.pallas_call(
        functools.partial(_combine_kernel, alpha=alpha),
        grid=(nb,),
        in_specs=[pl.BlockSpec((None, TOP_K, tm), lambda i: (i, 0, 0), memory_space=pltpu.SMEM),
                  pl.BlockSpec((tm, d), row), pl.BlockSpec((tm, TOP_K), row),
                  pl.BlockSpec((1, d), const), pl.BlockSpec((1, d), const),
                  pl.BlockSpec(memory_space=pl.ANY)],
        out_specs=[pl.BlockSpec((tm, d), row), pl.BlockSpec((tm, d), row)],
        out_shape=[jax.ShapeDtypeStruct((n, d), F32), jax.ShapeDtypeStruct((n, d), BF16)],
        scratch_shapes=[pltpu.VMEM((TOP_K, tm, d), F32), pltpu.SemaphoreType.DMA(())],
        compiler_params=_cparams(("arbitrary",)),
        name="combine",
    )(dest, h, gates_cols, ln_g.reshape(1, d), ln_b.reshape(1, d), ys)


def moe_block(h, mix, ln_g, ln_b, router_wt, router_bias, w_gate, w_up, w_down, layer, *, alpha):
    n, d = h.shape
    cap = n + EXP_TILE
    h1, dest, gates, cnt = ln_router(h, mix, ln_g[0], ln_b[0], router_wt, router_bias,
                                          alpha=alpha, cap=cap)
    counts = cnt[:, 0]
    xs = dispatch(h1, dest, counts, cap=cap)
    ys = expert_ffn(xs, w_gate, w_up, w_down, layer, counts, cap=cap, n_tokens=n)
    gates_cols = jnp.transpose(gates, (0, 2, 1)).reshape(n, TOP_K)
    return combine(h1, dest, gates_cols, ln_g[1], ln_b[1], ys, alpha=alpha)


def _chunk_masks(n, chunk):
    ri = lax.broadcasted_iota(jnp.int32, (n, n), 0)
    ci = lax.broadcasted_iota(jnp.int32, (n, n), 1)
    same = (ri // chunk) == (ci // chunk)
    return same & (ri >= ci), same & (ri > ci)


def _gate_params_kernel(h_ref, w_ref, alog_ref, dtb_ref, o_ref, *, n_heads):
    tm = h_ref.shape[0]
    ab = _dot(h_ref[...], w_ref[...], precision=HIGHEST)
    g = -jnp.exp(alog_ref[...]) * jax.nn.softplus(ab + dtb_ref[...])
    beta = jax.nn.sigmoid(ab)
    causal, _ = _chunk_masks(tm, DN_CHUNK)
    gc = _dot(jnp.where(causal, 1.0, 0.0), g, precision=HIGHEST)
    lane = lax.broadcasted_iota(jnp.int32, ab.shape, 1)
    o_ref[...] = jnp.where(lane < n_heads, gc, beta)


def gate_params(h, w_ab, a_log, dt_bias):
    n, d = h.shape
    n_heads = a_log.shape[0]
    tm = min(AB_TILE, n)
    pad = V7X_LANES - 2 * n_heads
    w = jnp.pad(w_ab, ((0, 0), (0, pad)))
    lanes = lambda v: jnp.pad(v.astype(F32), (0, V7X_LANES - n_heads)).reshape(1, V7X_LANES)
    const = lambda i: (0, 0)
    return pl.pallas_call(
        functools.partial(_gate_params_kernel, n_heads=n_heads),
        grid=(n // tm,),
        in_specs=[pl.BlockSpec((tm, d), lambda i: (i, 0)), pl.BlockSpec((d, V7X_LANES), const),
                  pl.BlockSpec((1, V7X_LANES), const), pl.BlockSpec((1, V7X_LANES), const)],
        out_specs=pl.BlockSpec((tm, V7X_LANES), lambda i: (i, 0)),
        out_shape=jax.ShapeDtypeStruct((n, V7X_LANES), F32),
        compiler_params=_cparams(("parallel",)),
        name="gate_params",
    )(h, w, lanes(a_log), lanes(dt_bias))


def _unit_lower_inverse(a, precision):
    n = a.shape[0]
    ri = lax.broadcasted_iota(jnp.int32, (n, n), 0)
    ci = lax.broadcasted_iota(jnp.int32, (n, n), 1)
    p = -a
    inv = jnp.where(ri == ci, 1.0, 0.0) + p
    span = 2
    while span < DN_CHUNK:
        p = _dot(p, p, precision=precision)
        inv = inv + _dot(inv, p, precision=precision)
        span *= 2
    return inv


def _deltanet_kernel(q_ref, k_ref, v_ref, z_ref, cq_ref, ck_ref, cv_ref, gb_ref, gt_ref, ng_ref, y_any,
                     o_ref, s_ref, carry_ref, *, hb, dk, solve_precision):
    del y_any
    t = pl.program_id(2)
    tb = q_ref.shape[0]
    scale = dk ** -0.5

    @pl.when(t == 0)
    def _():
        s_ref[...] = jnp.zeros_like(s_ref)
        carry_ref[...] = jnp.zeros_like(carry_ref)

    def conv_silu(x_ref, w_ref, slot):
        raw = x_ref[...].astype(F32)
        ext = jnp.concatenate([carry_ref[slot], raw], axis=0)
        w = w_ref[...]
        taps = w.shape[0]
        acc = None
        for j in range(taps):
            off = V7X_SUBLANES - (taps - 1) + j
            term = ext[off:off + tb] * w[j:j + 1, :]
            acc = term if acc is None else acc + term
        carry_ref[slot] = raw[tb - V7X_SUBLANES:tb]
        return jax.nn.silu(acc)

    qa = conv_silu(q_ref, cq_ref, 0)
    ka = conv_silu(k_ref, ck_ref, 1)
    va = conv_silu(v_ref, cv_ref, 2)
    za = z_ref[...].astype(F32)
    gbl = gb_ref[...]
    gt = gt_ref[...]
    causal, strict = _chunk_masks(tb, DN_CHUNK)

    for hh in range(hb):
        sl = slice(hh * dk, (hh + 1) * dk)
        q, k, v = qa[:, sl], ka[:, sl], va[:, sl]
        qn = q * lax.rsqrt(jnp.sum(q * q, axis=-1, keepdims=True) + RMS_EPS) * scale
        kn = k * lax.rsqrt(jnp.sum(k * k, axis=-1, keepdims=True) + RMS_EPS)
        gc_col = gbl[:, hh:hh + 1]
        beta = gbl[:, hb + hh:hb + hh + 1]
        gc_row = gt[hh:hh + 1, :]
        decay = jnp.where(causal, jnp.exp(jnp.where(causal, gc_col - gc_row, 0.0)), 0.0)
        kb = kn * beta
        knb = kn.astype(BF16)
        a = jnp.where(strict, _nt(kb.astype(BF16), knb) * decay, 0.0)
        tinv = _unit_lower_inverse(a, solve_precision)
        eg = jnp.exp(gc_col)
        rhs = jnp.concatenate([v * beta, kb * eg], axis=1)
        sol = _dot(tinv, rhs, precision=solve_precision)
        u, w = sol[:, :dk], sol[:, dk:]
        qk = jnp.where(causal, _nt(qn.astype(BF16), knb) * decay, 0.0)
        qd = qn * eg
        s = s_ref[hh]
        outs = []
        for c in range(tb // DN_CHUNK):
            lo, hi = c * DN_CHUNK, (c + 1) * DN_CHUNK
            g_last = gc_col[hi - 1:hi, :]
            kd = kn[lo:hi] * jnp.exp(g_last - gc_col[lo:hi])
            sb = s.astype(BF16)
            v_new = u[lo:hi] - _dot(w[lo:hi].astype(BF16), sb)
            vb = v_new.astype(BF16)
            outs.append(_dot(qd[lo:hi].astype(BF16), sb) + _dot(qk[lo:hi, lo:hi].astype(BF16), vb))
            s = s * jnp.exp(g_last) + _tn(kd.astype(BF16), vb)
        s_ref[hh] = s
        o = jnp.concatenate(outs, axis=0)
        o = o * lax.rsqrt(jnp.mean(o * o, axis=-1, keepdims=True) + RMS_EPS) * ng_ref[...]
        o = o * jax.nn.silu(za[:, sl])
        o_ref[:, sl] = o.astype(o_ref.dtype)


def deltanet(qkvz, conv_w, gparams, norm_g, y, *, batch, n_heads, solve_precision=HIGHEST):
    n = qkvz.shape[0]
    dk = norm_g.shape[0]
    hb = min(DN_HB, n_heads)
    w = hb * dk
    nhb = n_heads // hb
    seq = n // batch
    tb = min(DN_TB, seq)
    nt = seq // tb
    gc = gparams[:, :n_heads].reshape(n, nhb, hb)
    beta = gparams[:, n_heads:2 * n_heads].reshape(n, nhb, hb)
    cols = jnp.concatenate([gc, beta], axis=-1).transpose(1, 0, 2)
    cols = jnp.pad(cols, ((0, 0), (0, 0), (0, V7X_LANES - 2 * hb)))
    rows = jnp.pad(gc.transpose(1, 2, 0), ((0, 0), (0, V7X_SUBLANES - hb), (0, 0)))
    rmap = lambda part: (lambda b, h, t: (b * nt + t, part * nhb + h))
    cmap = lambda part: (lambda b, h, t: (0, part * nhb + h))
    taps = conv_w.shape[0]
    return pl.pallas_call(
        functools.partial(_deltanet_kernel, hb=hb, dk=dk, solve_precision=solve_precision),
        grid=(batch, nhb, nt),
        in_specs=[pl.BlockSpec((tb, w), rmap(0)), pl.BlockSpec((tb, w), rmap(1)),
                  pl.BlockSpec((tb, w), rmap(2)), pl.BlockSpec((tb, w), rmap(3)),
                  pl.BlockSpec((taps, w), cmap(0)), pl.BlockSpec((taps, w), cmap(1)),
                  pl.BlockSpec((taps, w), cmap(2)),
                  pl.BlockSpec((None, tb, V7X_LANES), lambda b, h, t: (h, b * nt + t, 0)),
                  pl.BlockSpec((None, V7X_SUBLANES, tb), lambda b, h, t: (h, 0, b * nt + t)),
                  pl.BlockSpec((1, dk), lambda b, h, t: (0, 0)),
                  pl.BlockSpec(memory_space=pl.ANY)],
        out_specs=pl.BlockSpec((tb, w), rmap(0)),
        out_shape=jax.ShapeDtypeStruct(y.shape, y.dtype),
        scratch_shapes=[pltpu.VMEM((hb, dk, dk), F32), pltpu.VMEM((3, V7X_SUBLANES, w), F32)],
        input_output_aliases={10: 0},
        compiler_params=_cparams(("parallel", "parallel", "arbitrary")),
        name="deltanet",
    )(qkvz, qkvz, qkvz, qkvz, conv_w, conv_w, conv_w, cols, rows, norm_g.reshape(1, dk), y)


POOL_HALO = 16


def _pool_kernel(x_ref, w_ref, b_ref, sc_ref, o_ref, carry_ref):
    t = pl.program_id(1)
    tt = x_ref.shape[0]
    cg = w_ref.shape[1]

    @pl.when(t == 0)
    def _():
        carry_ref[...] = jnp.zeros_like(carry_ref)

    x = x_ref[...].astype(F32)
    ext = jnp.concatenate([carry_ref[...], x], axis=0)
    carry_ref[...] = x[tt - POOL_HALO:tt]
    pos = t * tt + lax.broadcasted_iota(jnp.int32, (tt, 1), 0)
    for gi, win in enumerate(POOL_WINDOWS):
        cols = slice(gi * cg, (gi + 1) * cg)
        cur = ext[:, cols]
        span = 1
        while span < win:
            cur = cur[span:] + cur[:cur.shape[0] - span]
            span *= 2
        lo = POOL_HALO - (win - 1)
        total = cur[lo:lo + tt]
        count = jnp.minimum(pos + 1, win).astype(F32)
        pooled = total / count - x[:, cols]
        mixed = _dot(pooled.astype(BF16), w_ref[gi]) + b_ref[:, cols]
        o_ref[:, cols] = (mixed * sc_ref[:, cols]).astype(o_ref.dtype)


def pool(xp, pool_w, pool_b, pool_scale, *, batch, out_width):
    n, p = xp.shape
    assert out_width % p == 0 and max(POOL_WINDOWS) <= POOL_HALO
    seq = n // batch
    tt = min(POOL_TILE, seq)
    nt = seq // tt
    g, cg, _ = pool_w.shape
    const = lambda b, t: (0, 0)
    return pl.pallas_call(
        _pool_kernel,
        grid=(batch, nt),
        in_specs=[pl.BlockSpec((tt, p), lambda b, t: (b * nt + t, 0)),
                  pl.BlockSpec((g, cg, cg), lambda b, t: (0, 0, 0)),
                  pl.BlockSpec((1, p), const), pl.BlockSpec((1, p), const)],
        out_specs=pl.BlockSpec((tt, p), lambda b, t: (b * nt + t, out_width // p - 1)),
        out_shape=jax.ShapeDtypeStruct((n, out_width), BF16),
        scratch_shapes=[pltpu.VMEM((POOL_HALO, p), F32)],
        compiler_params=_cparams(("parallel", "arbitrary")),
        name="pool",
    )(xp, pool_w, pool_b.reshape(1, p).astype(F32), pool_scale.reshape(1, p).astype(F32))


def _sgu_kernel(u_ref, v_ref, g_ref, b_ref, ws_ref, bs_ref, o_ref):
    tt, width = u_ref.shape
    n_heads, chunk, _ = ws_ref.shape
    hd = width // n_heads
    vn = _layer_norm_rows(v_ref[...].astype(F32), g_ref[...], b_ref[...]).astype(BF16)
    ri = lax.broadcasted_iota(jnp.int32, (chunk, chunk), 0)
    ci = lax.broadcasted_iota(jnp.int32, (chunk, chunk), 1)
    for g in range(n_heads):
        cols = slice(g * hd, (g + 1) * hd)
        wsg = jnp.where(ri >= ci, ws_ref[g], 0.0).astype(BF16)
        bias = bs_ref[:, g:g + 1]
        for c in range(tt // chunk):
            rows = slice(c * chunk, (c + 1) * chunk)
            mixed = _dot(wsg, vn[rows, cols]) + bias
            o_ref[rows, cols] = (u_ref[rows, cols].astype(F32) * mixed).astype(o_ref.dtype)


def sgu(uv, ln_g, ln_b, w_s, b_s):
    n, two_w = uv.shape
    width = two_w // 2
    n_heads, chunk, _ = w_s.shape
    tt = min(SGU_TILE, n)
    assert tt % chunk == 0
    const = lambda i: (0, 0)
    return pl.pallas_call(
        _sgu_kernel,
        grid=(n // tt,),
        in_specs=[pl.BlockSpec((tt, width), lambda i: (i, 0)), pl.BlockSpec((tt, width), lambda i: (i, 1)),
                  pl.BlockSpec((1, width), const), pl.BlockSpec((1, width), const),
                  pl.BlockSpec((n_heads, chunk, chunk), lambda i: (0, 0, 0)),
                  pl.BlockSpec((chunk, n_heads), const)],
        out_specs=pl.BlockSpec((tt, width), lambda i: (i, 0)),
        out_shape=jax.ShapeDtypeStruct((n, width), BF16),
        compiler_params=_cparams(("parallel",)),
        name="sgu",
    )(uv, uv, ln_g.reshape(1, width), ln_b.reshape(1, width), w_s, jnp.transpose(b_s))


def even_mixer(hf, hb16, w_in, w_conv, a_log, dt_bias, norm_g, pool_w, pool_b, pool_scale, w_out, *, batch):
    n_heads = a_log.shape[0]
    dk = norm_g.shape[0]
    dn_width = n_heads * dk
    o0 = 4 * dn_width
    o1 = o0 + 2 * n_heads
    qkvz = matmul(hb16, w_in[:, :o0].astype(BF16))
    xp = matmul(hb16, w_in[:, o1:].astype(BF16))
    gparams = gate_params(hf, w_in[:, o0:o1], a_log, dt_bias)
    y = pool(xp, pool_w.astype(BF16), pool_b, pool_scale, batch=batch, out_width=w_out.shape[0])
    y = deltanet(qkvz, w_conv, gparams, norm_g, y, batch=batch, n_heads=n_heads)
    return matmul(y, w_out.astype(BF16))


def odd_mixer(hb16, w_in, b_in, ln_g, ln_b, w_s, b_s, w_out):
    uv = matmul(hb16, w_in.astype(BF16), bias=b_in, act="gelu")
    gated = sgu(uv, ln_g, ln_b, w_s, b_s)
    return matmul(gated, w_out.astype(BF16))


def kernel(x, ev_w_in, ev_conv, ev_a_log, ev_dt_bias, ev_norm_g, ev_pool_w, ev_pool_b, ev_pool_scale, ev_w_out, od_w_in, od_b_in, od_ln_g, od_ln_b, od_w_s, od_b_s, od_w_out, ln_g, ln_b, router_w, router_bias, moe_w_gate, moe_w_up, moe_w_down):
    batch, seq, d = x.shape
    depth = ln_g.shape[0]
    alpha = (2.0 * depth) ** 0.25
    hf = x.reshape(batch * seq, d)
    hb16 = hf.astype(BF16)
    router_wt = jnp.transpose(router_w)
    wg, wu, wd = moe_w_gate.astype(BF16), moe_w_up.astype(BF16), moe_w_down.astype(BF16)
    for layer in range(depth):
        i = layer // 2
        if layer % 2 == 0:
            mix = even_mixer(hf, hb16, ev_w_in[i], ev_conv[i], ev_a_log[i], ev_dt_bias[i], ev_norm_g[i],
                             ev_pool_w[i], ev_pool_b[i], ev_pool_scale[i], ev_w_out[i], batch=batch)
        else:
            mix = odd_mixer(hb16, od_w_in[i], od_b_in[i], od_ln_g[i], od_ln_b[i], od_w_s[i], od_b_s[i],
                            od_w_out[i])
        hf, hb16 = moe_block(hf, mix, ln_g[layer], ln_b[layer], router_wt, router_bias, wg, wu, wd, layer,
                             alpha=alpha)
    return hf.reshape(batch, seq, d)
```

```python
import functools

import jax
import jax.numpy as jnp
from jax import lax
from jax.experimental import pallas as pl
from jax.experimental.pallas import tpu as pltpu

F32 = jnp.float32
BF16 = jnp.bfloat16
HIGHEST = lax.Precision.HIGHEST

V7X_LANES = 128
V7X_SUBLANES = 8
V7X_VMEM_BYTES = 64 * 1024 * 1024
VMEM_LIMIT = V7X_VMEM_BYTES - 8 * 1024 * 1024

DN_CHUNK = 64
POOL_WINDOWS = (2, 4, 8, 16)
N_GROUPS = 4
TOP_K = 2
LN_EPS = 1e-5
RMS_EPS = 1e-6

MM_TM = 1024
MM_TN = 1024
TOK_TILE = 256
EXP_TILE = 256
DN_TB = 256
DN_HB = 4
AB_TILE = 256
POOL_TILE = 256
SGU_TILE = 256


def _cparams(sem):
    return pltpu.CompilerParams(dimension_semantics=sem, vmem_limit_bytes=VMEM_LIMIT)


def _nt(a, b, precision=None):
    return lax.dot_general(a, b, (((1,), (1,)), ((), ())), precision=precision,
                           preferred_element_type=F32)


def _tn(a, b, precision=None):
    return lax.dot_general(a, b, (((0,), (0,)), ((), ())), precision=precision,
                           preferred_element_type=F32)


def _dot(a, b, precision=None):
    return jnp.dot(a, b, precision=precision, preferred_element_type=F32)


def _mm_kernel(*refs, act, has_bias):
    if has_bias:
        x_ref, w_ref, b_ref, o_ref = refs
    else:
        x_ref, w_ref, o_ref = refs
    acc = _dot(x_ref[...], w_ref[...])
    if has_bias:
        acc = acc + b_ref[...]
    if act == "gelu":
        acc = jax.nn.gelu(acc)
    o_ref[...] = acc.astype(o_ref.dtype)


def matmul(x, w, bias=None, act=None, out_dtype=BF16):
    m, k = x.shape
    n = w.shape[1]
    tm, tn = min(MM_TM, m), min(MM_TN, n)
    assert m % tm == 0 and n % tn == 0
    in_specs = [pl.BlockSpec((tm, k), lambda i, j: (i, 0)),
                pl.BlockSpec((k, tn), lambda i, j: (0, j))]
    args = [x, w]
    if bias is not None:
        in_specs.append(pl.BlockSpec((1, tn), lambda i, j: (0, j)))
        args.append(bias.reshape(1, n).astype(F32))
    return pl.pallas_call(
        functools.partial(_mm_kernel, act=act, has_bias=bias is not None),
        grid=(m // tm, n // tn),
        in_specs=in_specs,
        out_specs=pl.BlockSpec((tm, tn), lambda i, j: (i, j)),
        out_shape=jax.ShapeDtypeStruct((m, n), out_dtype),
        compiler_params=_cparams(("parallel", "arbitrary")),
        name="matmul",
    )(*args)


def _layer_norm_rows(x, g, b):
    mu = jnp.mean(x, axis=-1, keepdims=True)
    xc = x - mu
    var = jnp.mean(xc * xc, axis=-1, keepdims=True)
    return xc * lax.rsqrt(var + LN_EPS) * g + b


def _ln_router_kernel(h_ref, mix_ref, g_ref, b_ref, rwt_ref, rb_ref,
                      h1_ref, dest_ref, gate_ref, cnt_ref, carry_ref,
                      *, alpha, n_experts, cap):
    i = pl.program_id(0)
    epg = n_experts // N_GROUPS
    tm = h_ref.shape[0]

    @pl.when(i == 0)
    def _():
        carry_ref[...] = jnp.zeros_like(carry_ref)

    x = alpha * h_ref[...] + mix_ref[...].astype(F32)
    y = _layer_norm_rows(x, g_ref[...], b_ref[...])
    h1_ref[...] = y

    logits = _nt(rwt_ref[...], y, precision=HIGHEST)
    mx = jnp.max(logits, axis=0, keepdims=True)
    ex = jnp.exp(logits - mx)
    probs = ex / jnp.sum(ex, axis=0, keepdims=True)
    sel = probs + rb_ref[...]

    srow = [sel[e:e + 1, :] for e in range(n_experts)]
    prow = [probs[e:e + 1, :] for e in range(n_experts)]

    def top2_sum(vals):
        best = None
        for a in range(len(vals)):
            for c in range(a + 1, len(vals)):
                s = vals[a] + vals[c]
                best = s if best is None else jnp.maximum(best, s)
        return best

    gscore = [top2_sum(srow[g * epg:(g + 1) * epg]) for g in range(N_GROUPS)]
    best_g = jnp.zeros_like(gscore[0], dtype=jnp.int32)
    best_s = gscore[0]
    for g in range(1, N_GROUPS):
        upd = gscore[g] > best_s
        best_g = jnp.where(upd, g, best_g)
        best_s = jnp.where(upd, gscore[g], best_s)

    def pick(rows, j):
        out = rows[j]
        for g in range(1, N_GROUPS):
            out = jnp.where(best_g == g, rows[g * epg + j], out)
        return out

    in_s = [pick(srow, j) for j in range(epg)]
    in_p = [pick(prow, j) for j in range(epg)]

    def argmax_first(vals, exclude=None):
        bi = None
        bv = None
        for j, v in enumerate(vals):
            if exclude is not None:
                v = jnp.where(exclude == j, -jnp.inf, v)
            if bi is None:
                bi = jnp.zeros_like(best_g)
                bv = v
            else:
                upd = v > bv
                bi = jnp.where(upd, j, bi)
                bv = jnp.where(upd, v, bv)
        return bi

    loc0 = argmax_first(in_s)
    loc1 = argmax_first(in_s, exclude=loc0)

    def take(vals, idx):
        out = vals[0]
        for j in range(1, len(vals)):
            out = jnp.where(idx == j, vals[j], out)
        return out

    p0 = take(in_p, loc0)
    p1 = take(in_p, loc1)
    psum = p0 + p1
    gate_ref[...] = jnp.concatenate([p0 / psum, p1 / psum], axis=0)[None]

    idx0 = best_g * epg + loc0
    idx1 = best_g * epg + loc1

    eid = lax.broadcasted_iota(jnp.int32, (n_experts, tm), 0)
    hit0 = eid == idx0
    hit1 = eid == idx1
    member = jnp.where(hit0, 1.0, jnp.where(hit1, 1.0, 0.0))
    srci = lax.broadcasted_iota(jnp.int32, (tm, tm), 0)
    dsti = lax.broadcasted_iota(jnp.int32, (tm, tm), 1)
    upper = jnp.where(srci < dsti, 1.0, 0.0).astype(BF16)
    prefix = _dot(member.astype(BF16), upper)
    carry = carry_ref[...]
    pos = eid.astype(F32) * float(cap) + carry[:, :1] + prefix
    dest0 = jnp.sum(jnp.where(hit0, pos, 0.0), axis=0, keepdims=True)
    dest1 = jnp.sum(jnp.where(hit1, pos, 0.0), axis=0, keepdims=True)
    dest_ref[...] = jnp.concatenate([dest0, dest1], axis=0).astype(jnp.int32)[None]
    carry = carry + jnp.sum(member, axis=1, keepdims=True)
    carry_ref[...] = carry
    cnt_ref[...] = carry.astype(jnp.int32)


def ln_router(h, mix, ln_g, ln_b, router_wt, router_bias, *, alpha, cap):
    n, d = h.shape
    n_experts = router_wt.shape[0]
    tm = min(TOK_TILE, n)
    nb = n // tm
    row = lambda i: (i, 0)
    const = lambda i: (0, 0)
    return pl.pallas_call(
        functools.partial(_ln_router_kernel, alpha=alpha, n_experts=n_experts, cap=cap),
        grid=(nb,),
        in_specs=[pl.BlockSpec((tm, d), row), pl.BlockSpec((tm, d), row),
                  pl.BlockSpec((1, d), const), pl.BlockSpec((1, d), const),
                  pl.BlockSpec((n_experts, d), const), pl.BlockSpec((n_experts, 1), const)],
        out_specs=[pl.BlockSpec((tm, d), row),
                   pl.BlockSpec((1, TOP_K, tm), lambda i: (i, 0, 0)),
                   pl.BlockSpec((1, TOP_K, tm), lambda i: (i, 0, 0)),
                   pl.BlockSpec((n_experts, V7X_LANES), const)],
        out_shape=[jax.ShapeDtypeStruct((n, d), F32),
                   jax.ShapeDtypeStruct((nb, TOP_K, tm), jnp.int32),
                   jax.ShapeDtypeStruct((nb, TOP_K, tm), F32),
                   jax.ShapeDtypeStruct((n_experts, V7X_LANES), jnp.int32)],
        scratch_shapes=[pltpu.VMEM((n_experts, V7X_LANES), F32)],
        compiler_params=_cparams(("arbitrary",)),
        name="ln_router",
    )(h, mix, ln_g.reshape(1, d), ln_b.reshape(1, d), router_wt, router_bias.reshape(n_experts, 1))


def _row_copy(src_ref, src_row, dst_ref, dst_row, sem):
    return pltpu.make_async_copy(src_ref.at[pl.ds(src_row, 1), :], dst_ref.at[pl.ds(dst_row, 1), :], sem)


def _dispatch_kernel(cnt_ref, dest_ref, x_ref, xs_ref, zero_ref, sem, zsem, *, cap, n_experts):
    i = pl.program_id(0)
    tm = x_ref.shape[0]

    def start(r, c):
        for k in range(TOP_K):
            _row_copy(x_ref, r, xs_ref, dest_ref[k, r], sem).start()
        return c

    lax.fori_loop(0, tm, start, 0)

    @pl.when(i == pl.num_programs(0) - 1)
    def _():
        zero_ref[...] = jnp.zeros_like(zero_ref)

        def pad_rows(e, do):
            cnt = cnt_ref[e]
            n_pad = lax.rem(EXP_TILE - lax.rem(cnt, EXP_TILE), EXP_TILE)

            def one(r, c):
                do(_row_copy(zero_ref, 0, xs_ref, e * cap + cnt + r, zsem))
                return c

            lax.fori_loop(0, n_pad, one, 0)
            return do

        lax.fori_loop(0, n_experts, lambda e, c: (pad_rows(e, lambda cp: cp.start()), c)[1], 0)
        lax.fori_loop(0, n_experts, lambda e, c: (pad_rows(e, lambda cp: cp.wait()), c)[1], 0)

    def wait(r, c):
        for k in range(TOP_K):
            _row_copy(x_ref, 0, xs_ref, 0, sem).wait()
        return c

    lax.fori_loop(0, tm, wait, 0)


def dispatch(x, dest, counts, *, cap):
    n, d = x.shape
    n_experts = counts.shape[0]
    nb, _, tm = dest.shape
    return pl.pallas_call(
        functools.partial(_dispatch_kernel, cap=cap, n_experts=n_experts),
        grid_spec=pltpu.PrefetchScalarGridSpec(
            num_scalar_prefetch=1,
            grid=(nb,),
            in_specs=[pl.BlockSpec((None, TOP_K, tm), lambda i, c: (i, 0, 0), memory_space=pltpu.SMEM),
                      pl.BlockSpec((tm, d), lambda i, c: (i, 0))],
            out_specs=pl.BlockSpec(memory_space=pl.ANY),
            scratch_shapes=[pltpu.VMEM((V7X_SUBLANES, d), x.dtype),
                            pltpu.SemaphoreType.DMA(()), pltpu.SemaphoreType.DMA(())],
        ),
        out_shape=jax.ShapeDtypeStruct((n_experts * cap, d), x.dtype),
        compiler_params=_cparams(("arbitrary",)),
        name="dispatch",
    )(counts, dest, x)


def _expert_kernel(blk_ref, exp_ref, valid_ref, x_ref, wg_ref, wu_ref, wd_ref, o_ref):
    s = pl.program_id(0)

    @pl.when(valid_ref[s] == 1)
    def _():
        x = x_ref[...].astype(BF16)
        gate = _dot(x, wg_ref[...])
        up = _dot(x, wu_ref[...])
        act = (jax.nn.silu(gate) * up).astype(BF16)
        o_ref[...] = _dot(act, wd_ref[...]).astype(o_ref.dtype)


def expert_ffn(xs, w_gate, w_up, w_down, layer, counts, *, cap, n_tokens):
    d = xs.shape[1]
    n_experts, f = w_gate.shape[1], w_gate.shape[3]
    tm = EXP_TILE
    n_tiles = (TOP_K * n_tokens) // tm + n_experts
    tiles_e = (counts + tm - 1) // tm
    cum = jnp.cumsum(tiles_e)
    total = cum[-1]
    slot = jnp.minimum(jnp.arange(n_tiles, dtype=jnp.int32), total - 1)
    e_of = jnp.sum((slot[:, None] >= cum[None, :]).astype(jnp.int32), axis=1)
    first = jnp.take(cum - tiles_e, e_of)
    blk = (e_of * (cap // tm) + slot - first).astype(jnp.int32)
    valid = (jnp.arange(n_tiles) < total).astype(jnp.int32)
    xmap = lambda s, blk, ex, va: (blk[s], 0)
    return pl.pallas_call(
        _expert_kernel,
        grid_spec=pltpu.PrefetchScalarGridSpec(
            num_scalar_prefetch=3,
            grid=(n_tiles,),
            in_specs=[pl.BlockSpec((tm, d), xmap),
                      pl.BlockSpec((None, None, d, f), lambda s, blk, ex, va: (layer, ex[s], 0, 0)),
                      pl.BlockSpec((None, None, d, f), lambda s, blk, ex, va: (layer, ex[s], 0, 0)),
                      pl.BlockSpec((None, None, f, d), lambda s, blk, ex, va: (layer, ex[s], 0, 0))],
            out_specs=pl.BlockSpec((tm, d), xmap),
        ),
        out_shape=jax.ShapeDtypeStruct(xs.shape, F32),
        compiler_params=_cparams(("arbitrary",)),
        name="expert_ffn",
    )(blk, e_of.astype(jnp.int32), valid, xs, w_gate, w_up, w_down)


def _combine_kernel(dest_ref, next_dest_ref, h_ref, gate_ref, g_ref, b_ref, ys_ref, o_ref, ob_ref, buf_ref, sems,
                    *, alpha):
    i = pl.program_id(0)
    tm = h_ref.shape[0]
    slot = lax.rem(i, 2)

    def gather(idx_ref, to_slot):
        def start(r, c):
            for k in range(TOP_K):
                _row_copy(ys_ref, idx_ref[k, r], buf_ref.at[to_slot, k], r, sems.at[to_slot]).start()
            return c

        lax.fori_loop(0, tm, start, 0)

    @pl.when(i == 0)
    def _():
        gather(dest_ref, 0)

    @pl.when(i + 1 < pl.num_programs(0))
    def _():
        gather(next_dest_ref, 1 - slot)

    def wait(r, c):
        for k in range(TOP_K):
            _row_copy(ys_ref, 0, buf_ref.at[slot, k], 0, sems.at[slot]).wait()
        return c

    lax.fori_loop(0, tm, wait, 0)

    gates = gate_ref[...]
    ffn = gates[:, 0:1] * buf_ref[slot, 0] + gates[:, 1:2] * buf_ref[slot, 1]
    x = alpha * h_ref[...] + ffn
    y = _layer_norm_rows(x, g_ref[...], b_ref[...])
    o_ref[...] = y
    ob_ref[...] = y.astype(BF16)


def combine(h, dest, gates_cols, ln_g, ln_b, ys, *, alpha):
    n, d = h.shape
    nb, _, tm = dest.shape
    row = lambda i: (i, 0)
    const = lambda i: (0, 0)
    return pl.pallas_call(
        functools.partial(_combine_kernel, alpha=alpha),
        grid=(nb,),
        in_specs=[pl.BlockSpec((None, TOP_K, tm), lambda i: (i, 0, 0), memory_space=pltpu.SMEM),
                  pl.BlockSpec((None, TOP_K, tm), lambda i: (jnp.minimum(i + 1, nb - 1), 0, 0),
                               memory_space=pltpu.SMEM),
                  pl.BlockSpec((tm, d), row), pl.BlockSpec((tm, TOP_K), row),
                  pl.BlockSpec((1, d), const), pl.BlockSpec((1, d), const),
                  pl.BlockSpec(memory_space=pl.ANY)],
        out_specs=[pl.BlockSpec((tm, d), row), pl.BlockSpec((tm, d), row)],
        out_shape=[jax.ShapeDtypeStruct((n, d), F32), jax.ShapeDtypeStruct((n, d), BF16)],
        scratch_shapes=[pltpu.VMEM((2, TOP_K, tm, d), F32), pltpu.SemaphoreType.DMA((2,))],
        compiler_params=_cparams(("arbitrary",)),
        name="combine",
    )(dest, dest, h, gates_cols, ln_g.reshape(1, d), ln_b.reshape(1, d), ys)


def moe_block(h, mix, ln_g, ln_b, router_wt, router_bias, w_gate, w_up, w_down, layer, *, alpha):
    n, d = h.shape
    cap = n + EXP_TILE
    h1, dest, gates, cnt = ln_router(h, mix, ln_g[0], ln_b[0], router_wt, router_bias,
                                          alpha=alpha, cap=cap)
    counts = cnt[:, 0]
    xs = dispatch(h1, dest, counts, cap=cap)
    ys = expert_ffn(xs, w_gate, w_up, w_down, layer, counts, cap=cap, n_tokens=n)
    gates_cols = jnp.transpose(gates, (0, 2, 1)).reshape(n, TOP_K)
    return combine(h1, dest, gates_cols, ln_g[1], ln_b[1], ys, alpha=alpha)


def _chunk_masks(n, chunk):
    ri = lax.broadcasted_iota(jnp.int32, (n, n), 0)
    ci = lax.broadcasted_iota(jnp.int32, (n, n), 1)
    same = (ri // chunk) == (ci // chunk)
    return same & (ri >= ci), same & (ri > ci)


def _gate_params_kernel(h_ref, w_ref, alog_ref, dtb_ref, o_ref, *, n_heads):
    tm = h_ref.shape[0]
    ab = _dot(h_ref[...], w_ref[...], precision=HIGHEST)
    g = -jnp.exp(alog_ref[...]) * jax.nn.softplus(ab + dtb_ref[...])
    beta = jax.nn.sigmoid(ab)
    causal, _ = _chunk_masks(tm, DN_CHUNK)
    gc = _dot(jnp.where(causal, 1.0, 0.0), g, precision=HIGHEST)
    lane = lax.broadcasted_iota(jnp.int32, ab.shape, 1)
    o_ref[...] = jnp.where(lane < n_heads, gc, beta)


def gate_params(h, w_ab, a_log, dt_bias):
    n, d = h.shape
    n_heads = a_log.shape[0]
    tm = min(AB_TILE, n)
    pad = V7X_LANES - 2 * n_heads
    w = jnp.pad(w_ab, ((0, 0), (0, pad)))
    lanes = lambda v: jnp.pad(v.astype(F32), (0, V7X_LANES - n_heads)).reshape(1, V7X_LANES)
    const = lambda i: (0, 0)
    return pl.pallas_call(
        functools.partial(_gate_params_kernel, n_heads=n_heads),
        grid=(n // tm,),
        in_specs=[pl.BlockSpec((tm, d), lambda i: (i, 0)), pl.BlockSpec((d, V7X_LANES), const),
                  pl.BlockSpec((1, V7X_LANES), const), pl.BlockSpec((1, V7X_LANES), const)],
        out_specs=pl.BlockSpec((tm, V7X_LANES), lambda i: (i, 0)),
        out_shape=jax.ShapeDtypeStruct((n, V7X_LANES), F32),
        compiler_params=_cparams(("parallel",)),
        name="gate_params",
    )(h, w, lanes(a_log), lanes(dt_bias))


def _unit_lower_inverses(mats):
    n = mats[0].shape[0]
    ri = lax.broadcasted_iota(jnp.int32, (n, n), 0)
    ci = lax.broadcasted_iota(jnp.int32, (n, n), 1)
    differ = ri ^ ci
    eye = jnp.where(ri == ci, 1.0, 0.0)
    base = (differ >> 1) == 0
    invs = [eye - jnp.where(base, a, 0.0) for a in mats]
    level = 1
    while (1 << level) < DN_CHUNK:
        at_level = (differ >> level) == 1
        invbs = [inv.astype(BF16) for inv in invs]
        halves = [_dot(jnp.where(at_level, a, 0.0).astype(BF16), invb).astype(BF16)
                  for a, invb in zip(mats, invbs)]
        invs = [inv - _dot(invb, half) for inv, invb, half in zip(invs, invbs, halves)]
        level += 1
    return invs


def _deltanet_kernel(q_ref, k_ref, v_ref, z_ref, cq_ref, ck_ref, cv_ref, gb_ref, gt_ref, ng_ref, y_any,
                     o_ref, s_ref, carry_ref, *, hb, dk):
    del y_any
    t = pl.program_id(2)
    tb = q_ref.shape[0]
    scale = dk ** -0.5

    @pl.when(t == 0)
    def _():
        s_ref[...] = jnp.zeros_like(s_ref)
        carry_ref[...] = jnp.zeros_like(carry_ref)

    def conv_silu(x_ref, w_ref, slot):
        raw = x_ref[...].astype(F32)
        ext = jnp.concatenate([carry_ref[slot], raw], axis=0)
        w = w_ref[...]
        taps = w.shape[0]
        acc = None
        for j in range(taps):
            lag = taps - 1 - j
            window = pltpu.roll(ext, lag, axis=0)[V7X_SUBLANES:] if lag else raw
            term = window * w[j:j + 1, :]
            acc = term if acc is None else acc + term
        carry_ref[slot] = raw[tb - V7X_SUBLANES:tb]
        return jax.nn.silu(acc)

    qa = conv_silu(q_ref, cq_ref, 0)
    ka = conv_silu(k_ref, ck_ref, 1)
    va = conv_silu(v_ref, cv_ref, 2)
    za = z_ref[...].astype(F32)
    gbl = gb_ref[...]
    gt = gt_ref[...]
    causal, strict = _chunk_masks(tb, DN_CHUNK)

    heads = range(hb)
    n_chunks = tb // DN_CHUNK
    qn, kn, gc_col, a, rhs, qk = [], [], [], [], [], []
    for hh in heads:
        sl = slice(hh * dk, (hh + 1) * dk)
        q, k, v = qa[:, sl], ka[:, sl], va[:, sl]
        qn.append(q * lax.rsqrt(jnp.sum(q * q, axis=-1, keepdims=True) + RMS_EPS) * scale)
        kn.append(k * lax.rsqrt(jnp.sum(k * k, axis=-1, keepdims=True) + RMS_EPS))
        gc_col.append(gbl[:, hh:hh + 1])
        beta = gbl[:, hb + hh:hb + hh + 1]
        gc_row = gt[hh:hh + 1, :]
        decay = jnp.where(causal, jnp.exp(jnp.where(causal, gc_col[hh] - gc_row, 0.0)), 0.0)
        kb = kn[hh] * beta
        knb = kn[hh].astype(BF16)
        a.append(jnp.where(strict, _nt(kb.astype(BF16), knb) * decay, 0.0))
        qk.append(jnp.where(causal, _nt(qn[hh].astype(BF16), knb) * decay, 0.0).astype(BF16))
        rhs.append(jnp.concatenate([v * beta, kb * jnp.exp(gc_col[hh])], axis=1).astype(BF16))
    tinv = _unit_lower_inverses(a)
    solb = [_dot(tinv[hh].astype(BF16), rhs[hh]).astype(BF16) for hh in heads]
    qk_sol = [_dot(qk[hh], solb[hh]) for hh in heads]
    q_eff = [(qn[hh] * jnp.exp(gc_col[hh]) - qk_sol[hh][:, dk:]).astype(BF16) for hh in heads]
    g_last = [[gc_col[hh][(c + 1) * DN_CHUNK - 1:(c + 1) * DN_CHUNK, :] for c in range(n_chunks)] for hh in heads]
    kd_sol = [[_tn((kn[hh][c * DN_CHUNK:(c + 1) * DN_CHUNK]
                    * jnp.exp(g_last[hh][c] - gc_col[hh][c * DN_CHUNK:(c + 1) * DN_CHUNK])).astype(BF16),
                   solb[hh][c * DN_CHUNK:(c + 1) * DN_CHUNK])
               for c in range(n_chunks)] for hh in heads]
    s = [s_ref[hh] for hh in heads]
    outs = [[] for _ in heads]
    for c in range(n_chunks):
        lo, hi = c * DN_CHUNK, (c + 1) * DN_CHUNK
        for hh in heads:
            sb = s[hh].astype(BF16)
            outs[hh].append(qk_sol[hh][lo:hi, :dk] + _dot(q_eff[hh][lo:hi], sb))
            s[hh] = (s[hh] * jnp.exp(g_last[hh][c]) - _dot(kd_sol[hh][c][:, dk:].astype(BF16), sb)
                     + kd_sol[hh][c][:, :dk])
    for hh in heads:
        s_ref[hh] = s[hh]
        o = jnp.concatenate(outs[hh], axis=0)
        o = o * lax.rsqrt(jnp.mean(o * o, axis=-1, keepdims=True) + RMS_EPS) * ng_ref[...]
        o = o * jax.nn.silu(za[:, hh * dk:(hh + 1) * dk])
        o_ref[:, hh * dk:(hh + 1) * dk] = o.astype(o_ref.dtype)


def deltanet(qkvz, conv_w, gparams, norm_g, y, *, batch, n_heads):
    n = qkvz.shape[0]
    dk = norm_g.shape[0]
    hb = min(DN_HB, n_heads)
    w = hb * dk
    nhb = n_heads // hb
    seq = n // batch
    tb = min(DN_TB, seq)
    nt = seq // tb
    gc = gparams[:, :n_heads].reshape(n, nhb, hb)
    beta = gparams[:, n_heads:2 * n_heads].reshape(n, nhb, hb)
    cols = jnp.concatenate([gc, beta], axis=-1).transpose(1, 0, 2)
    cols = jnp.pad(cols, ((0, 0), (0, 0), (0, V7X_LANES - 2 * hb)))
    rows = jnp.pad(gc.transpose(1, 2, 0), ((0, 0), (0, V7X_SUBLANES - hb), (0, 0)))
    rmap = lambda part: (lambda b, h, t: (b * nt + t, part * nhb + h))
    cmap = lambda part: (lambda b, h, t: (0, part * nhb + h))
    taps = conv_w.shape[0]
    return pl.pallas_call(
        functools.partial(_deltanet_kernel, hb=hb, dk=dk),
        grid=(batch, nhb, nt),
        in_specs=[pl.BlockSpec((tb, w), rmap(0)), pl.BlockSpec((tb, w), rmap(1)),
                  pl.BlockSpec((tb, w), rmap(2)), pl.BlockSpec((tb, w), rmap(3)),
                  pl.BlockSpec((taps, w), cmap(0)), pl.BlockSpec((taps, w), cmap(1)),
                  pl.BlockSpec((taps, w), cmap(2)),
                  pl.BlockSpec((None, tb, V7X_LANES), lambda b, h, t: (h, b * nt + t, 0)),
                  pl.BlockSpec((None, V7X_SUBLANES, tb), lambda b, h, t: (h, 0, b * nt + t)),
                  pl.BlockSpec((1, dk), lambda b, h, t: (0, 0)),
                  pl.BlockSpec(memory_space=pl.ANY)],
        out_specs=pl.BlockSpec((tb, w), rmap(0)),
        out_shape=jax.ShapeDtypeStruct(y.shape, y.dtype),
        scratch_shapes=[pltpu.VMEM((hb, dk, dk), F32), pltpu.VMEM((3, V7X_SUBLANES, w), F32)],
        input_output_aliases={10: 0},
        compiler_params=_cparams(("parallel", "parallel", "arbitrary")),
        name="deltanet",
    )(qkvz, qkvz, qkvz, qkvz, conv_w, conv_w, conv_w, cols, rows, norm_g.reshape(1, dk), y)


POOL_HALO = 16


def _pool_kernel(x_ref, w_ref, b_ref, sc_ref, o_ref, carry_ref):
    t = pl.program_id(1)
    tt = x_ref.shape[0]
    cg = w_ref.shape[1]

    @pl.when(t == 0)
    def _():
        carry_ref[...] = jnp.zeros_like(carry_ref)

    x = x_ref[...].astype(F32)
    ext = jnp.concatenate([carry_ref[...], x], axis=0)
    carry_ref[...] = x[tt - POOL_HALO:tt]
    pos = t * tt + lax.broadcasted_iota(jnp.int32, (tt, 1), 0)
    for gi, win in enumerate(POOL_WINDOWS):
        cols = slice(gi * cg, (gi + 1) * cg)
        cur = ext[:, cols]
        span = 1
        while span < win:
            cur = cur[span:] + cur[:cur.shape[0] - span]
            span *= 2
        lo = POOL_HALO - (win - 1)
        total = cur[lo:lo + tt]
        count = jnp.minimum(pos + 1, win).astype(F32)
        pooled = total / count - x[:, cols]
        mixed = _dot(pooled.astype(BF16), w_ref[gi]) + b_ref[:, cols]
        o_ref[:, cols] = (mixed * sc_ref[:, cols]).astype(o_ref.dtype)


def pool(xp, pool_w, pool_b, pool_scale, *, batch, out_width):
    n, p = xp.shape
    assert out_width % p == 0 and max(POOL_WINDOWS) <= POOL_HALO
    seq = n // batch
    tt = min(POOL_TILE, seq)
    nt = seq // tt
    g, cg, _ = pool_w.shape
    const = lambda b, t: (0, 0)
    return pl.pallas_call(
        _pool_kernel,
        grid=(batch, nt),
        in_specs=[pl.BlockSpec((tt, p), lambda b, t: (b * nt + t, 0)),
                  pl.BlockSpec((g, cg, cg), lambda b, t: (0, 0, 0)),
                  pl.BlockSpec((1, p), const), pl.BlockSpec((1, p), const)],
        out_specs=pl.BlockSpec((tt, p), lambda b, t: (b * nt + t, out_width // p - 1)),
        out_shape=jax.ShapeDtypeStruct((n, out_width), BF16),
        scratch_shapes=[pltpu.VMEM((POOL_HALO, p), F32)],
        compiler_params=_cparams(("parallel", "arbitrary")),
        name="pool",
    )(xp, pool_w, pool_b.reshape(1, p).astype(F32), pool_scale.reshape(1, p).astype(F32))


def _sgu_kernel(u_ref, v_ref, g_ref, b_ref, ws_ref, bs_ref, o_ref):
    tt, width = u_ref.shape
    n_heads, chunk, _ = ws_ref.shape
    hd = width // n_heads
    vn = _layer_norm_rows(v_ref[...].astype(F32), g_ref[...], b_ref[...]).astype(BF16)
    ri = lax.broadcasted_iota(jnp.int32, (chunk, chunk), 0)
    ci = lax.broadcasted_iota(jnp.int32, (chunk, chunk), 1)
    for g in range(n_heads):
        cols = slice(g * hd, (g + 1) * hd)
        wsg = jnp.where(ri >= ci, ws_ref[g], 0.0).astype(BF16)
        bias = bs_ref[:, g:g + 1]
        for c in range(tt // chunk):
            rows = slice(c * chunk, (c + 1) * chunk)
            mixed = _dot(wsg, vn[rows, cols]) + bias
            o_ref[rows, cols] = (u_ref[rows, cols].astype(F32) * mixed).astype(o_ref.dtype)


def sgu(uv, ln_g, ln_b, w_s, b_s):
    n, two_w = uv.shape
    width = two_w // 2
    n_heads, chunk, _ = w_s.shape
    tt = min(SGU_TILE, n)
    assert tt % chunk == 0
    const = lambda i: (0, 0)
    return pl.pallas_call(
        _sgu_kernel,
        grid=(n // tt,),
        in_specs=[pl.BlockSpec((tt, width), lambda i: (i, 0)), pl.BlockSpec((tt, width), lambda i: (i, 1)),
                  pl.BlockSpec((1, width), const), pl.BlockSpec((1, width), const),
                  pl.BlockSpec((n_heads, chunk, chunk), lambda i: (0, 0, 0)),
                  pl.BlockSpec((chunk, n_heads), const)],
        out_specs=pl.BlockSpec((tt, width), lambda i: (i, 0)),
        out_shape=jax.ShapeDtypeStruct((n, width), BF16),
        compiler_params=_cparams(("parallel",)),
        name="sgu",
    )(uv, uv, ln_g.reshape(1, width), ln_b.reshape(1, width), w_s, jnp.transpose(b_s))


def even_mixer(hf, hb16, w_in, w_conv, a_log, dt_bias, norm_g, pool_w, pool_b, pool_scale, w_out, *, batch):
    n_heads = a_log.shape[0]
    dk = norm_g.shape[0]
    dn_width = n_heads * dk
    o0 = 4 * dn_width
    o1 = o0 + 2 * n_heads
    qkvz = matmul(hb16, w_in[:, :o0].astype(BF16))
    xp = matmul(hb16, w_in[:, o1:].astype(BF16))
    gparams = gate_params(hf, w_in[:, o0:o1], a_log, dt_bias)
    y = pool(xp, pool_w.astype(BF16), pool_b, pool_scale, batch=batch, out_width=w_out.shape[0])
    y = deltanet(qkvz, w_conv, gparams, norm_g, y, batch=batch, n_heads=n_heads)
    return matmul(y, w_out.astype(BF16))


def odd_mixer(hb16, w_in, b_in, ln_g, ln_b, w_s, b_s, w_out):
    uv = matmul(hb16, w_in.astype(BF16), bias=b_in, act="gelu")
    gated = sgu(uv, ln_g, ln_b, w_s, b_s)
    return matmul(gated, w_out.astype(BF16))


def kernel(x, ev_w_in, ev_conv, ev_a_log, ev_dt_bias, ev_norm_g, ev_pool_w, ev_pool_b, ev_pool_scale, ev_w_out, od_w_in, od_b_in, od_ln_g, od_ln_b, od_w_s, od_b_s, od_w_out, ln_g, ln_b, router_w, router_bias, moe_w_gate, moe_w_up, moe_w_down):
    batch, seq, d = x.shape
    depth = ln_g.shape[0]
    alpha = (2.0 * depth) ** 0.25
    hf = x.reshape(batch * seq, d)
    hb16 = hf.astype(BF16)
    router_wt = jnp.transpose(router_w)
    wg, wu, wd = moe_w_gate.astype(BF16), moe_w_up.astype(BF16), moe_w_down.astype(BF16)
    for layer in range(depth):
        i = layer // 2
        if layer % 2 == 0:
            mix = even_mixer(hf, hb16, ev_w_in[i], ev_conv[i], ev_a_log[i], ev_dt_bias[i], ev_norm_g[i],
                             ev_pool_w[i], ev_pool_b[i], ev_pool_scale[i], ev_w_out[i], batch=batch)
        else:
            mix = odd_mixer(hb16, od_w_in[i], od_b_in[i], od_ln_g[i], od_ln_b[i], od_w_s[i], od_b_s[i],
                            od_w_out[i])
        hf, hb16 = moe_block(hf, mix, ln_g[layer], ln_b[layer], router_wt, router_bias, wg, wu, wd, layer,
                             alpha=alpha)
    return hf.reshape(batch, seq, d)
```

```python
import functools

import jax
import jax.numpy as jnp
from jax import lax
from jax.experimental import pallas as pl
from jax.experimental.pallas import tpu as pltpu

F32 = jnp.float32
BF16 = jnp.bfloat16

V7X_LANES = 128
V7X_SUBLANES = 8
V7X_VMEM_BYTES = 64 * 1024 * 1024
VMEM_LIMIT = V7X_VMEM_BYTES - 8 * 1024 * 1024

DN_CHUNK = 64
POOL_WINDOWS = (2, 4, 8, 16)
N_GROUPS = 4
TOP_K = 2
LN_EPS = 1e-5
RMS_EPS = 1e-6

MM_TM = 1024
MM_TN = 1024
TOK_TILE = 256
EXP_TILE = 512
DN_TB = 256
DN_HB = 4
AB_TILE = 256
POOL_TILE = 256
SGU_TILE = 256


def _cparams(sem):
    return pltpu.CompilerParams(dimension_semantics=sem, vmem_limit_bytes=VMEM_LIMIT)


def _nt(a, b):
    return lax.dot_general(a, b, (((1,), (1,)), ((), ())), preferred_element_type=F32)


def _tn(a, b):
    return lax.dot_general(a, b, (((0,), (0,)), ((), ())), preferred_element_type=F32)


def _dot(a, b):
    return jnp.dot(a, b, preferred_element_type=F32)


def _split_bf16(x):
    hi = x.astype(BF16)
    return hi, (x - hi.astype(F32)).astype(BF16)


def _nt_split(a, b):
    a_hi, a_lo = _split_bf16(a)
    b_hi, b_lo = _split_bf16(b)
    return _nt(a_hi, b_hi) + (_nt(a_hi, b_lo) + _nt(a_lo, b_hi))


def _mm_kernel(*refs, act, has_bias):
    if has_bias:
        x_ref, w_ref, b_ref, o_ref = refs
    else:
        x_ref, w_ref, o_ref = refs
    acc = _dot(x_ref[...], w_ref[...])
    if has_bias:
        acc = acc + b_ref[...]
    if act == "gelu":
        acc = jax.nn.gelu(acc)
    o_ref[...] = acc.astype(o_ref.dtype)


def matmul(x, w, bias=None, act=None, out_dtype=BF16):
    m, k = x.shape
    n = w.shape[1]
    tm, tn = min(MM_TM, m), min(MM_TN, n)
    assert m % tm == 0 and n % tn == 0
    in_specs = [pl.BlockSpec((tm, k), lambda i, j: (i, 0)),
                pl.BlockSpec((k, tn), lambda i, j: (0, j))]
    args = [x, w]
    if bias is not None:
        in_specs.append(pl.BlockSpec((1, tn), lambda i, j: (0, j)))
        args.append(bias.reshape(1, n).astype(F32))
    return pl.pallas_call(
        functools.partial(_mm_kernel, act=act, has_bias=bias is not None),
        grid=(m // tm, n // tn),
        in_specs=in_specs,
        out_specs=pl.BlockSpec((tm, tn), lambda i, j: (i, j)),
        out_shape=jax.ShapeDtypeStruct((m, n), out_dtype),
        compiler_params=_cparams(("parallel", "arbitrary")),
        name="matmul",
    )(*args)


def _pack_rows(x):
    half = x.shape[1] // 2
    bits = lambda v: lax.bitcast_convert_type(v.astype(BF16).astype(F32), jnp.uint32)
    return bits(x[:, :half]) | (bits(x[:, half:]) >> 16)


def _unpack_rows(words):
    hi = lax.bitcast_convert_type(words & jnp.uint32(0xFFFF0000), F32)
    lo = lax.bitcast_convert_type(words << 16, F32)
    return hi, lo


def _layer_norm_rows(x, g, b):
    mu = jnp.mean(x, axis=-1, keepdims=True)
    xc = x - mu
    var = jnp.mean(xc * xc, axis=-1, keepdims=True)
    return xc * lax.rsqrt(var + LN_EPS) * g + b


def _ln_router_kernel(h_ref, mix_ref, g_ref, b_ref, rwt_ref, rb_ref,
                      h1_ref, h1p_ref, dest_ref, gate_ref, cnt_ref, carry_ref,
                      *, alpha, n_experts, cap):
    i = pl.program_id(0)
    epg = n_experts // N_GROUPS
    tm = h_ref.shape[0]

    @pl.when(i == 0)
    def _():
        carry_ref[...] = jnp.zeros_like(carry_ref)

    x = alpha * h_ref[...] + mix_ref[...].astype(F32)
    y = _layer_norm_rows(x, g_ref[...], b_ref[...])
    h1_ref[...] = y
    h1p_ref[...] = _pack_rows(y)

    logits = _nt_split(rwt_ref[...], y)
    mx = jnp.max(logits, axis=0, keepdims=True)
    ex = jnp.exp(logits - mx)
    probs = ex / jnp.sum(ex, axis=0, keepdims=True)
    sel = probs + rb_ref[...]

    srow = [sel[e:e + 1, :] for e in range(n_experts)]
    prow = [probs[e:e + 1, :] for e in range(n_experts)]

    def top2_sum(vals):
        best = None
        for a in range(len(vals)):
            for c in range(a + 1, len(vals)):
                s = vals[a] + vals[c]
                best = s if best is None else jnp.maximum(best, s)
        return best

    gscore = [top2_sum(srow[g * epg:(g + 1) * epg]) for g in range(N_GROUPS)]
    best_g = jnp.zeros_like(gscore[0], dtype=jnp.int32)
    best_s = gscore[0]
    for g in range(1, N_GROUPS):
        upd = gscore[g] > best_s
        best_g = jnp.where(upd, g, best_g)
        best_s = jnp.where(upd, gscore[g], best_s)

    def pick(rows, j):
        out = rows[j]
        for g in range(1, N_GROUPS):
            out = jnp.where(best_g == g, rows[g * epg + j], out)
        return out

    in_s = [pick(srow, j) for j in range(epg)]
    in_p = [pick(prow, j) for j in range(epg)]

    def argmax_first(vals, exclude=None):
        bi = None
        bv = None
        for j, v in enumerate(vals):
            if exclude is not None:
                v = jnp.where(exclude == j, -jnp.inf, v)
            if bi is None:
                bi = jnp.zeros_like(best_g)
                bv = v
            else:
                upd = v > bv
                bi = jnp.where(upd, j, bi)
                bv = jnp.where(upd, v, bv)
        return bi

    loc0 = argmax_first(in_s)
    loc1 = argmax_first(in_s, exclude=loc0)

    def take(vals, idx):
        out = vals[0]
        for j in range(1, len(vals)):
            out = jnp.where(idx == j, vals[j], out)
        return out

    p0 = take(in_p, loc0)
    p1 = take(in_p, loc1)
    psum = p0 + p1
    gate_ref[...] = jnp.concatenate([p0 / psum, p1 / psum], axis=0)[None]

    idx0 = best_g * epg + loc0
    idx1 = best_g * epg + loc1

    eid = lax.broadcasted_iota(jnp.int32, (n_experts, tm), 0)
    hit0 = eid == idx0
    hit1 = eid == idx1
    member = jnp.where(hit0, 1.0, jnp.where(hit1, 1.0, 0.0))
    srci = lax.broadcasted_iota(jnp.int32, (tm, tm), 0)
    dsti = lax.broadcasted_iota(jnp.int32, (tm, tm), 1)
    upper = jnp.where(srci < dsti, 1.0, 0.0).astype(BF16)
    prefix = _dot(member.astype(BF16), upper)
    carry = carry_ref[...]
    pos = eid.astype(F32) * float(cap) + carry[:, :1] + prefix
    dest0 = jnp.sum(jnp.where(hit0, pos, 0.0), axis=0, keepdims=True)
    dest1 = jnp.sum(jnp.where(hit1, pos, 0.0), axis=0, keepdims=True)
    dest_ref[...] = jnp.concatenate([dest0, dest1], axis=0).astype(jnp.int32)[None]
    carry = carry + jnp.sum(member, axis=1, keepdims=True)
    carry_ref[...] = carry
    cnt_ref[...] = carry.astype(jnp.int32)


def ln_router(h, mix, ln_g, ln_b, router_wt, router_bias, *, alpha, cap):
    n, d = h.shape
    n_experts = router_wt.shape[0]
    tm = min(TOK_TILE, n)
    nb = n // tm
    row = lambda i: (i, 0)
    const = lambda i: (0, 0)
    return pl.pallas_call(
        functools.partial(_ln_router_kernel, alpha=alpha, n_experts=n_experts, cap=cap),
        grid=(nb,),
        in_specs=[pl.BlockSpec((tm, d), row), pl.BlockSpec((tm, d), row),
                  pl.BlockSpec((1, d), const), pl.BlockSpec((1, d), const),
                  pl.BlockSpec((n_experts, d), const), pl.BlockSpec((n_experts, 1), const)],
        out_specs=[pl.BlockSpec((tm, d), row), pl.BlockSpec((tm, d // 2), row),
                   pl.BlockSpec((1, TOP_K, tm), lambda i: (i, 0, 0)),
                   pl.BlockSpec((1, TOP_K, tm), lambda i: (i, 0, 0)),
                   pl.BlockSpec((n_experts, V7X_LANES), const)],
        out_shape=[jax.ShapeDtypeStruct((n, d), F32), jax.ShapeDtypeStruct((n, d // 2), jnp.uint32),
                   jax.ShapeDtypeStruct((nb, TOP_K, tm), jnp.int32),
                   jax.ShapeDtypeStruct((nb, TOP_K, tm), F32),
                   jax.ShapeDtypeStruct((n_experts, V7X_LANES), jnp.int32)],
        scratch_shapes=[pltpu.VMEM((n_experts, V7X_LANES), F32)],
        compiler_params=_cparams(("arbitrary",)),
        name="ln_router",
    )(h, mix, ln_g.reshape(1, d), ln_b.reshape(1, d), router_wt, router_bias.reshape(n_experts, 1))


def _row_copy(src_ref, src_row, dst_ref, dst_row, sem):
    return pltpu.make_async_copy(src_ref.at[pl.ds(src_row, 1), :], dst_ref.at[pl.ds(dst_row, 1), :], sem)


def _dispatch_kernel(cnt_ref, dest_ref, x_ref, xs_ref, zero_ref, sem, zsem, *, cap, n_experts):
    i = pl.program_id(0)
    tm = x_ref.shape[0]

    def start(r, c):
        for k in range(TOP_K):
            _row_copy(x_ref, r, xs_ref, dest_ref[k, r], sem).start()
        return c

    lax.fori_loop(0, tm, start, 0)

    @pl.when(i == pl.num_programs(0) - 1)
    def _():
        zero_ref[...] = jnp.zeros_like(zero_ref)

        def pad_rows(e, do):
            cnt = cnt_ref[e]
            n_pad = lax.rem(EXP_TILE - lax.rem(cnt, EXP_TILE), EXP_TILE)

            def one(r, c):
                do(_row_copy(zero_ref, 0, xs_ref, e * cap + cnt + r, zsem))
                return c

            lax.fori_loop(0, n_pad, one, 0)
            return do

        lax.fori_loop(0, n_experts, lambda e, c: (pad_rows(e, lambda cp: cp.start()), c)[1], 0)
        lax.fori_loop(0, n_experts, lambda e, c: (pad_rows(e, lambda cp: cp.wait()), c)[1], 0)

    def wait(r, c):
        for k in range(TOP_K):
            _row_copy(x_ref, 0, xs_ref, 0, sem).wait()
        return c

    lax.fori_loop(0, tm, wait, 0)


def dispatch(x, dest, counts, *, cap):
    n, d = x.shape
    n_experts = counts.shape[0]
    nb, _, tm = dest.shape
    return pl.pallas_call(
        functools.partial(_dispatch_kernel, cap=cap, n_experts=n_experts),
        grid_spec=pltpu.PrefetchScalarGridSpec(
            num_scalar_prefetch=1,
            grid=(nb,),
            in_specs=[pl.BlockSpec((None, TOP_K, tm), lambda i, c: (i, 0, 0), memory_space=pltpu.SMEM),
                      pl.BlockSpec((tm, d), lambda i, c: (i, 0))],
            out_specs=pl.BlockSpec(memory_space=pl.ANY),
            scratch_shapes=[pltpu.VMEM((V7X_SUBLANES, d), x.dtype),
                            pltpu.SemaphoreType.DMA(()), pltpu.SemaphoreType.DMA(())],
        ),
        out_shape=jax.ShapeDtypeStruct((n_experts * cap, d), x.dtype),
        compiler_params=_cparams(("arbitrary",)),
        name="dispatch",
    )(counts, dest, x)


def _expert_kernel(blk_ref, exp_ref, valid_ref, x_ref, wg_ref, wu_ref, wd_ref, o_ref):
    s = pl.program_id(0)

    @pl.when(valid_ref[s] == 1)
    def _():
        hi, lo = _unpack_rows(x_ref[...])
        x = jnp.concatenate([hi.astype(BF16), lo.astype(BF16)], axis=1)
        gate = _dot(x, wg_ref[...])
        up = _dot(x, wu_ref[...])
        act = (jax.nn.silu(gate) * up).astype(BF16)
        o_ref[...] = _pack_rows(_dot(act, wd_ref[...]))


def expert_ffn(xs, w_gate, w_up, w_down, layer, counts, *, cap, n_tokens):
    d = xs.shape[1]
    n_experts, f = w_gate.shape[1], w_gate.shape[3]
    tm = EXP_TILE
    n_tiles = (TOP_K * n_tokens) // tm + n_experts
    tiles_e = (counts + tm - 1) // tm
    cum = jnp.cumsum(tiles_e)
    total = cum[-1]
    slot = jnp.minimum(jnp.arange(n_tiles, dtype=jnp.int32), total - 1)
    e_of = jnp.sum((slot[:, None] >= cum[None, :]).astype(jnp.int32), axis=1)
    first = jnp.take(cum - tiles_e, e_of)
    blk = (e_of * (cap // tm) + slot - first).astype(jnp.int32)
    valid = (jnp.arange(n_tiles) < total).astype(jnp.int32)
    xmap = lambda s, blk, ex, va: (blk[s], 0)
    return pl.pallas_call(
        _expert_kernel,
        grid_spec=pltpu.PrefetchScalarGridSpec(
            num_scalar_prefetch=3,
            grid=(n_tiles,),
            in_specs=[pl.BlockSpec((tm, d), xmap),
                      pl.BlockSpec((None, None, 2 * d, f), lambda s, blk, ex, va: (layer, ex[s], 0, 0)),
                      pl.BlockSpec((None, None, 2 * d, f), lambda s, blk, ex, va: (layer, ex[s], 0, 0)),
                      pl.BlockSpec((None, None, f, 2 * d), lambda s, blk, ex, va: (layer, ex[s], 0, 0))],
            out_specs=pl.BlockSpec((tm, d), xmap),
        ),
        out_shape=jax.ShapeDtypeStruct(xs.shape, jnp.uint32),
        compiler_params=_cparams(("arbitrary",)),
        name="expert_ffn",
    )(blk, e_of.astype(jnp.int32), valid, xs, w_gate, w_up, w_down)


def _combine_kernel(dest_ref, next_dest_ref, h_ref, gate_ref, g_ref, b_ref, ys_ref, o_ref, ob_ref, buf_ref, sems,
                    *, alpha):
    i = pl.program_id(0)
    tm = h_ref.shape[0]
    slot = lax.rem(i, 2)

    def gather(idx_ref, to_slot):
        def start(r, c):
            for k in range(TOP_K):
                _row_copy(ys_ref, idx_ref[k, r], buf_ref.at[to_slot, k], r, sems.at[to_slot]).start()
            return c

        lax.fori_loop(0, tm, start, 0)

    @pl.when(i == 0)
    def _():
        gather(dest_ref, 0)

    @pl.when(i + 1 < pl.num_programs(0))
    def _():
        gather(next_dest_ref, 1 - slot)

    def wait(r, c):
        for k in range(TOP_K):
            _row_copy(ys_ref, 0, buf_ref.at[slot, k], 0, sems.at[slot]).wait()
        return c

    lax.fori_loop(0, tm, wait, 0)

    gates = gate_ref[...]
    hi0, lo0 = _unpack_rows(buf_ref[slot, 0])
    hi1, lo1 = _unpack_rows(buf_ref[slot, 1])
    ffn = jnp.concatenate([gates[:, 0:1] * hi0 + gates[:, 1:2] * hi1,
                           gates[:, 0:1] * lo0 + gates[:, 1:2] * lo1], axis=1)
    x = alpha * h_ref[...] + ffn
    y = _layer_norm_rows(x, g_ref[...], b_ref[...])
    o_ref[...] = y
    ob_ref[...] = y.astype(BF16)


def combine(h, dest, gates_cols, ln_g, ln_b, ys, *, alpha):
    n, d = h.shape
    nb, _, tm = dest.shape
    row = lambda i: (i, 0)
    const = lambda i: (0, 0)
    return pl.pallas_call(
        functools.partial(_combine_kernel, alpha=alpha),
        grid=(nb,),
        in_specs=[pl.BlockSpec((None, TOP_K, tm), lambda i: (i, 0, 0), memory_space=pltpu.SMEM),
                  pl.BlockSpec((None, TOP_K, tm), lambda i: (jnp.minimum(i + 1, nb - 1), 0, 0),
                               memory_space=pltpu.SMEM),
                  pl.BlockSpec((tm, d), row), pl.BlockSpec((tm, TOP_K), row),
                  pl.BlockSpec((1, d), const), pl.BlockSpec((1, d), const),
                  pl.BlockSpec(memory_space=pl.ANY)],
        out_specs=[pl.BlockSpec((tm, d), row), pl.BlockSpec((tm, d), row)],
        out_shape=[jax.ShapeDtypeStruct((n, d), F32), jax.ShapeDtypeStruct((n, d), BF16)],
        scratch_shapes=[pltpu.VMEM((2, TOP_K, tm, d // 2), jnp.uint32), pltpu.SemaphoreType.DMA((2,))],
        compiler_params=_cparams(("arbitrary",)),
        name="combine",
    )(dest, dest, h, gates_cols, ln_g.reshape(1, d), ln_b.reshape(1, d), ys)


def moe_block(h, mix, ln_g, ln_b, router_wt, router_bias, w_gate, w_up, w_down, layer, *, alpha):
    n, d = h.shape
    cap = n + EXP_TILE
    h1, h1p, dest, gates, cnt = ln_router(h, mix, ln_g[0], ln_b[0], router_wt, router_bias,
                                          alpha=alpha, cap=cap)
    counts = cnt[:, 0]
    xs = dispatch(h1p, dest, counts, cap=cap)
    ys = expert_ffn(xs, w_gate, w_up, w_down, layer, counts, cap=cap, n_tokens=n)
    gates_cols = jnp.transpose(gates, (0, 2, 1)).reshape(n, TOP_K)
    return combine(h1, dest, gates_cols, ln_g[1], ln_b[1], ys, alpha=alpha)


def _chunk_masks(n, chunk):
    ri = lax.broadcasted_iota(jnp.int32, (n, n), 0)
    ci = lax.broadcasted_iota(jnp.int32, (n, n), 1)
    same = (ri // chunk) == (ci // chunk)
    return same & (ri >= ci), same & (ri > ci)


def _gate_params_kernel(h_ref, w_ref, alog_ref, dtb_ref, o_ref, *, n_heads):
    tm = h_ref.shape[0]
    h_hi, h_lo = _split_bf16(h_ref[...])
    w_hi, w_lo = _split_bf16(w_ref[...])
    ab = _dot(h_hi, w_hi) + (_dot(h_hi, w_lo) + _dot(h_lo, w_hi))
    g = -jnp.exp(alog_ref[...]) * jax.nn.softplus(ab + dtb_ref[...])
    beta = jax.nn.sigmoid(ab)
    causal, _ = _chunk_masks(tm, DN_CHUNK)
    tri = jnp.where(causal, 1.0, 0.0).astype(BF16)
    g_hi, g_rest = _split_bf16(g)
    g_mid, g_lo = _split_bf16(g - g_hi.astype(F32))
    del g_rest
    gc = _dot(tri, g_hi) + (_dot(tri, g_mid) + _dot(tri, g_lo))
    lane = lax.broadcasted_iota(jnp.int32, ab.shape, 1)
    o_ref[...] = jnp.where(lane < n_heads, gc, beta)


def gate_params(h, w_ab, a_log, dt_bias):
    n, d = h.shape
    n_heads = a_log.shape[0]
    tm = min(AB_TILE, n)
    pad = V7X_LANES - 2 * n_heads
    w = jnp.pad(w_ab, ((0, 0), (0, pad)))
    lanes = lambda v: jnp.pad(v.astype(F32), (0, V7X_LANES - n_heads)).reshape(1, V7X_LANES)
    const = lambda i: (0, 0)
    return pl.pallas_call(
        functools.partial(_gate_params_kernel, n_heads=n_heads),
        grid=(n // tm,),
        in_specs=[pl.BlockSpec((tm, d), lambda i: (i, 0)), pl.BlockSpec((d, V7X_LANES), const),
                  pl.BlockSpec((1, V7X_LANES), const), pl.BlockSpec((1, V7X_LANES), const)],
        out_specs=pl.BlockSpec((tm, V7X_LANES), lambda i: (i, 0)),
        out_shape=jax.ShapeDtypeStruct((n, V7X_LANES), F32),
        compiler_params=_cparams(("parallel",)),
        name="gate_params",
    )(h, w, lanes(a_log), lanes(dt_bias))


def _unit_lower_inverses(mats):
    n = mats[0].shape[0]
    ri = lax.broadcasted_iota(jnp.int32, (n, n), 0)
    ci = lax.broadcasted_iota(jnp.int32, (n, n), 1)
    differ = ri ^ ci
    eye = jnp.where(ri == ci, 1.0, 0.0)
    base = (differ >> 1) == 0
    invs = [eye - jnp.where(base, a, 0.0) for a in mats]
    level = 1
    while (1 << level) < DN_CHUNK:
        at_level = (differ >> level) == 1
        invbs = [inv.astype(BF16) for inv in invs]
        halves = [_dot(jnp.where(at_level, a, 0.0).astype(BF16), invb).astype(BF16)
                  for a, invb in zip(mats, invbs)]
        invs = [inv - _dot(invb, half) for inv, invb, half in zip(invs, invbs, halves)]
        level += 1
    return invs


def _deltanet_kernel(q_ref, k_ref, v_ref, z_ref, cq_ref, ck_ref, cv_ref, gb_ref, gt_ref, ng_ref, y_any,
                     o_ref, s_ref, carry_ref, *, hb, dk):
    del y_any
    t = pl.program_id(2)
    tb = q_ref.shape[0]
    scale = dk ** -0.5

    @pl.when(t == 0)
    def _():
        s_ref[...] = jnp.zeros_like(s_ref)
        carry_ref[...] = jnp.zeros_like(carry_ref)

    def conv_silu(x_ref, w_ref, slot):
        raw = x_ref[...].astype(F32)
        ext = jnp.concatenate([carry_ref[slot], raw], axis=0)
        w = w_ref[...]
        taps = w.shape[0]
        acc = None
        for j in range(taps):
            lag = taps - 1 - j
            window = pltpu.roll(ext, lag, axis=0)[V7X_SUBLANES:] if lag else raw
            term = window * w[j:j + 1, :]
            acc = term if acc is None else acc + term
        carry_ref[slot] = raw[tb - V7X_SUBLANES:tb]
        return jax.nn.silu(acc)

    qa = conv_silu(q_ref, cq_ref, 0)
    ka = conv_silu(k_ref, ck_ref, 1)
    va = conv_silu(v_ref, cv_ref, 2)
    za = z_ref[...].astype(F32)
    gbl = gb_ref[...]
    gt = gt_ref[...]
    causal, strict = _chunk_masks(tb, DN_CHUNK)

    heads = range(hb)
    n_chunks = tb // DN_CHUNK
    qn, kn, gc_col, a, rhs, qk = [], [], [], [], [], []
    for hh in heads:
        sl = slice(hh * dk, (hh + 1) * dk)
        q, k, v = qa[:, sl], ka[:, sl], va[:, sl]
        qn.append(q * lax.rsqrt(jnp.sum(q * q, axis=-1, keepdims=True) + RMS_EPS) * scale)
        kn.append(k * lax.rsqrt(jnp.sum(k * k, axis=-1, keepdims=True) + RMS_EPS))
        gc_col.append(gbl[:, hh:hh + 1])
        beta = gbl[:, hb + hh:hb + hh + 1]
        gc_row = gt[hh:hh + 1, :]
        decay = jnp.where(causal, jnp.exp(jnp.where(causal, gc_col[hh] - gc_row, 0.0)), 0.0)
        kb = kn[hh] * beta
        knb = kn[hh].astype(BF16)
        a.append(jnp.where(strict, _nt(kb.astype(BF16), knb) * decay, 0.0))
        qk.append(jnp.where(causal, _nt(qn[hh].astype(BF16), knb) * decay, 0.0).astype(BF16))
        rhs.append(jnp.concatenate([v * beta, kb * jnp.exp(gc_col[hh])], axis=1).astype(BF16))
    tinv = _unit_lower_inverses(a)
    solb = [_dot(tinv[hh].astype(BF16), rhs[hh]).astype(BF16) for hh in heads]
    qk_sol = [_dot(qk[hh], solb[hh]) for hh in heads]
    q_eff = [(qn[hh] * jnp.exp(gc_col[hh]) - qk_sol[hh][:, dk:]).astype(BF16) for hh in heads]
    g_last = [[gc_col[hh][(c + 1) * DN_CHUNK - 1:(c + 1) * DN_CHUNK, :] for c in range(n_chunks)] for hh in heads]
    kd_sol = [[_tn((kn[hh][c * DN_CHUNK:(c + 1) * DN_CHUNK]
                    * jnp.exp(g_last[hh][c] - gc_col[hh][c * DN_CHUNK:(c + 1) * DN_CHUNK])).astype(BF16),
                   solb[hh][c * DN_CHUNK:(c + 1) * DN_CHUNK])
               for c in range(n_chunks)] for hh in heads]
    s = [s_ref[hh] for hh in heads]
    outs = [[] for _ in heads]
    for c in range(n_chunks):
        lo, hi = c * DN_CHUNK, (c + 1) * DN_CHUNK
        for hh in heads:
            sb = s[hh].astype(BF16)
            outs[hh].append(qk_sol[hh][lo:hi, :dk] + _dot(q_eff[hh][lo:hi], sb))
            s[hh] = (s[hh] * jnp.exp(g_last[hh][c]) - _dot(kd_sol[hh][c][:, dk:].astype(BF16), sb)
                     + kd_sol[hh][c][:, :dk])
    for hh in heads:
        s_ref[hh] = s[hh]
        o = jnp.concatenate(outs[hh], axis=0)
        o = o * lax.rsqrt(jnp.mean(o * o, axis=-1, keepdims=True) + RMS_EPS) * ng_ref[...]
        o = o * jax.nn.silu(za[:, hh * dk:(hh + 1) * dk])
        o_ref[:, hh * dk:(hh + 1) * dk] = o.astype(o_ref.dtype)


def deltanet(qkvz, conv_w, gparams, norm_g, y, *, batch, n_heads):
    n = qkvz.shape[0]
    dk = norm_g.shape[0]
    hb = min(DN_HB, n_heads)
    w = hb * dk
    nhb = n_heads // hb
    seq = n // batch
    tb = min(DN_TB, seq)
    nt = seq // tb
    gc = gparams[:, :n_heads].reshape(n, nhb, hb)
    beta = gparams[:, n_heads:2 * n_heads].reshape(n, nhb, hb)
    cols = jnp.concatenate([gc, beta], axis=-1).transpose(1, 0, 2)
    cols = jnp.pad(cols, ((0, 0), (0, 0), (0, V7X_LANES - 2 * hb)))
    rows = jnp.pad(gc.transpose(1, 2, 0), ((0, 0), (0, V7X_SUBLANES - hb), (0, 0)))
    rmap = lambda part: (lambda b, h, t: (b * nt + t, part * nhb + h))
    cmap = lambda part: (lambda b, h, t: (0, part * nhb + h))
    taps = conv_w.shape[0]
    return pl.pallas_call(
        functools.partial(_deltanet_kernel, hb=hb, dk=dk),
        grid=(batch, nhb, nt),
        in_specs=[pl.BlockSpec((tb, w), rmap(0)), pl.BlockSpec((tb, w), rmap(1)),
                  pl.BlockSpec((tb, w), rmap(2)), pl.BlockSpec((tb, w), rmap(3)),
                  pl.BlockSpec((taps, w), cmap(0)), pl.BlockSpec((taps, w), cmap(1)),
                  pl.BlockSpec((taps, w), cmap(2)),
                  pl.BlockSpec((None, tb, V7X_LANES), lambda b, h, t: (h, b * nt + t, 0)),
                  pl.BlockSpec((None, V7X_SUBLANES, tb), lambda b, h, t: (h, 0, b * nt + t)),
                  pl.BlockSpec((1, dk), lambda b, h, t: (0, 0)),
                  pl.BlockSpec(memory_space=pl.ANY)],
        out_specs=pl.BlockSpec((tb, w), rmap(0)),
        out_shape=jax.ShapeDtypeStruct(y.shape, y.dtype),
        scratch_shapes=[pltpu.VMEM((hb, dk, dk), F32), pltpu.VMEM((3, V7X_SUBLANES, w), F32)],
        input_output_aliases={10: 0},
        compiler_params=_cparams(("parallel", "parallel", "arbitrary")),
        name="deltanet",
    )(qkvz, qkvz, qkvz, qkvz, conv_w, conv_w, conv_w, cols, rows, norm_g.reshape(1, dk), y)


POOL_HALO = 16


def _pool_kernel(x_ref, w_ref, b_ref, sc_ref, o_ref, carry_ref):
    t = pl.program_id(1)
    tt = x_ref.shape[0]
    cg = w_ref.shape[1]

    @pl.when(t == 0)
    def _():
        carry_ref[...] = jnp.zeros_like(carry_ref)

    x = x_ref[...].astype(F32)
    ext = jnp.concatenate([carry_ref[...], x], axis=0)
    carry_ref[...] = x[tt - POOL_HALO:tt]
    pos = t * tt + lax.broadcasted_iota(jnp.int32, (tt, 1), 0)
    for gi, win in enumerate(POOL_WINDOWS):
        cols = slice(gi * cg, (gi + 1) * cg)
        cur = ext[:, cols]
        span = 1
        while span < win:
            cur = cur[span:] + cur[:cur.shape[0] - span]
            span *= 2
        lo = POOL_HALO - (win - 1)
        total = cur[lo:lo + tt]
        count = jnp.minimum(pos + 1, win).astype(F32)
        pooled = total / count - x[:, cols]
        mixed = _dot(pooled.astype(BF16), w_ref[gi]) + b_ref[:, cols]
        o_ref[:, cols] = (mixed * sc_ref[:, cols]).astype(o_ref.dtype)


def pool(xp, pool_w, pool_b, pool_scale, *, batch, out_width):
    n, p = xp.shape
    assert out_width % p == 0 and max(POOL_WINDOWS) <= POOL_HALO
    seq = n // batch
    tt = min(POOL_TILE, seq)
    nt = seq // tt
    g, cg, _ = pool_w.shape
    const = lambda b, t: (0, 0)
    return pl.pallas_call(
        _pool_kernel,
        grid=(batch, nt),
        in_specs=[pl.BlockSpec((tt, p), lambda b, t: (b * nt + t, 0)),
                  pl.BlockSpec((g, cg, cg), lambda b, t: (0, 0, 0)),
                  pl.BlockSpec((1, p), const), pl.BlockSpec((1, p), const)],
        out_specs=pl.BlockSpec((tt, p), lambda b, t: (b * nt + t, out_width // p - 1)),
        out_shape=jax.ShapeDtypeStruct((n, out_width), BF16),
        scratch_shapes=[pltpu.VMEM((POOL_HALO, p), F32)],
        compiler_params=_cparams(("parallel", "arbitrary")),
        name="pool",
    )(xp, pool_w, pool_b.reshape(1, p).astype(F32), pool_scale.reshape(1, p).astype(F32))


def _sgu_kernel(u_ref, v_ref, g_ref, b_ref, ws_ref, bs_ref, o_ref):
    tt, width = u_ref.shape
    n_heads, chunk, _ = ws_ref.shape
    hd = width // n_heads
    vn = _layer_norm_rows(v_ref[...].astype(F32), g_ref[...], b_ref[...]).astype(BF16)
    ri = lax.broadcasted_iota(jnp.int32, (chunk, chunk), 0)
    ci = lax.broadcasted_iota(jnp.int32, (chunk, chunk), 1)
    for g in range(n_heads):
        cols = slice(g * hd, (g + 1) * hd)
        wsg = jnp.where(ri >= ci, ws_ref[g], 0.0).astype(BF16)
        bias = bs_ref[:, g:g + 1]
        for c in range(tt // chunk):
            rows = slice(c * chunk, (c + 1) * chunk)
            mixed = _dot(wsg, vn[rows, cols]) + bias
            o_ref[rows, cols] = (u_ref[rows, cols].astype(F32) * mixed).astype(o_ref.dtype)


def sgu(uv, ln_g, ln_b, w_s, b_s):
    n, two_w = uv.shape
    width = two_w // 2
    n_heads, chunk, _ = w_s.shape
    tt = min(SGU_TILE, n)
    assert tt % chunk == 0
    const = lambda i: (0, 0)
    return pl.pallas_call(
        _sgu_kernel,
        grid=(n // tt,),
        in_specs=[pl.BlockSpec((tt, width), lambda i: (i, 0)), pl.BlockSpec((tt, width), lambda i: (i, 1)),
                  pl.BlockSpec((1, width), const), pl.BlockSpec((1, width), const),
                  pl.BlockSpec((n_heads, chunk, chunk), lambda i: (0, 0, 0)),
                  pl.BlockSpec((chunk, n_heads), const)],
        out_specs=pl.BlockSpec((tt, width), lambda i: (i, 0)),
        out_shape=jax.ShapeDtypeStruct((n, width), BF16),
        compiler_params=_cparams(("parallel",)),
        name="sgu",
    )(uv, uv, ln_g.reshape(1, width), ln_b.reshape(1, width), w_s, jnp.transpose(b_s))


def even_mixer(hf, hb16, w_in, w_conv, a_log, dt_bias, norm_g, pool_w, pool_b, pool_scale, w_out, *, batch):
    n_heads = a_log.shape[0]
    dk = norm_g.shape[0]
    dn_width = n_heads * dk
    o0 = 4 * dn_width
    o1 = o0 + 2 * n_heads
    qkvz = matmul(hb16, w_in[:, :o0].astype(BF16))
    xp = matmul(hb16, w_in[:, o1:].astype(BF16))
    gparams = gate_params(hf, w_in[:, o0:o1], a_log, dt_bias)
    y = pool(xp, pool_w.astype(BF16), pool_b, pool_scale, batch=batch, out_width=w_out.shape[0])
    y = deltanet(qkvz, w_conv, gparams, norm_g, y, batch=batch, n_heads=n_heads)
    return matmul(y, w_out.astype(BF16))


def odd_mixer(hb16, w_in, b_in, ln_g, ln_b, w_s, b_s, w_out):
    uv = matmul(hb16, w_in.astype(BF16), bias=b_in, act="gelu")
    gated = sgu(uv, ln_g, ln_b, w_s, b_s)
    return matmul(gated, w_out.astype(BF16))


def kernel(x, ev_w_in, ev_conv, ev_a_log, ev_dt_bias, ev_norm_g, ev_pool_w, ev_pool_b, ev_pool_scale, ev_w_out, od_w_in, od_b_in, od_ln_g, od_ln_b, od_w_s, od_b_s, od_w_out, ln_g, ln_b, router_w, router_bias, moe_w_gate, moe_w_up, moe_w_down):
    batch, seq, d = x.shape
    depth = ln_g.shape[0]
    alpha = (2.0 * depth) ** 0.25
    hf = x.reshape(batch * seq, d)
    hb16 = hf.astype(BF16)
    router_wt = jnp.transpose(router_w)
    wg, wu, wd = moe_w_gate.astype(BF16), moe_w_up.astype(BF16), moe_w_down.astype(BF16)
    for layer in range(depth):
        i = layer // 2
        if layer % 2 == 0:
            mix = even_mixer(hf, hb16, ev_w_in[i], ev_conv[i], ev_a_log[i], ev_dt_bias[i], ev_norm_g[i],
                             ev_pool_w[i], ev_pool_b[i], ev_pool_scale[i], ev_w_out[i], batch=batch)
        else:
            mix = odd_mixer(hb16, od_w_in[i], od_b_in[i], od_ln_g[i], od_ln_b[i], od_w_s[i], od_b_s[i],
                            od_w_out[i])
        hf, hb16 = moe_block(hf, mix, ln_g[layer], ln_b[layer], router_wt, router_bias, wg, wu, wd, layer,
                             alpha=alpha)
    return hf.reshape(batch, seq, d)
```

```python
import functools

import jax
import jax.numpy as jnp
from jax import lax
from jax.experimental import pallas as pl
from jax.experimental.pallas import tpu as pltpu

F32 = jnp.float32
BF16 = jnp.bfloat16

V7X_LANES = 128
V7X_SUBLANES = 8
V7X_VMEM_BYTES = 64 * 1024 * 1024
VMEM_LIMIT = V7X_VMEM_BYTES - 8 * 1024 * 1024

DN_CHUNK = 64
POOL_WINDOWS = (2, 4, 8, 16)
N_GROUPS = 4
TOP_K = 2
LN_EPS = 1e-5
RMS_EPS = 1e-6

MM_TM = 1024
MM_TN = 1024
TOK_TILE = 256
EXP_TILE = 512
DN_TB = 256
DN_SOLVE = 128
DN_HB = 4
WAIT_UNROLL = 16
AB_TILE = 256
POOL_TILE = 256
SGU_TILE = 256


def _cparams(sem):
    return pltpu.CompilerParams(dimension_semantics=sem, vmem_limit_bytes=VMEM_LIMIT)


def _nt(a, b):
    return lax.dot_general(a, b, (((1,), (1,)), ((), ())), preferred_element_type=F32)


def _tn(a, b):
    return lax.dot_general(a, b, (((0,), (0,)), ((), ())), preferred_element_type=F32)


def _dot(a, b):
    return jnp.dot(a, b, preferred_element_type=F32)


def _split_bf16(x):
    hi = x.astype(BF16)
    return hi, (x - hi.astype(F32)).astype(BF16)


def _nt_split(a, b):
    a_hi, a_lo = _split_bf16(a)
    b_hi, b_lo = _split_bf16(b)
    return _nt(a_hi, b_hi) + (_nt(a_hi, b_lo) + _nt(a_lo, b_hi))


def _mm_kernel(*refs, act, has_bias):
    if has_bias:
        x_ref, w_ref, b_ref, o_ref = refs
    else:
        x_ref, w_ref, o_ref = refs
    acc = _dot(x_ref[...], w_ref[...])
    if has_bias:
        acc = acc + b_ref[...]
    if act == "gelu":
        acc = jax.nn.gelu(acc)
    o_ref[...] = acc.astype(o_ref.dtype)


def matmul(x, w, bias=None, act=None, out_dtype=BF16):
    m, k = x.shape
    n = w.shape[1]
    tm, tn = min(MM_TM, m), min(MM_TN, n)
    assert m % tm == 0 and n % tn == 0
    in_specs = [pl.BlockSpec((tm, k), lambda i, j: (i, 0)),
                pl.BlockSpec((k, tn), lambda i, j: (0, j))]
    args = [x, w]
    if bias is not None:
        in_specs.append(pl.BlockSpec((1, tn), lambda i, j: (0, j)))
        args.append(bias.reshape(1, n).astype(F32))
    return pl.pallas_call(
        functools.partial(_mm_kernel, act=act, has_bias=bias is not None),
        grid=(m // tm, n // tn),
        in_specs=in_specs,
        out_specs=pl.BlockSpec((tm, tn), lambda i, j: (i, j)),
        out_shape=jax.ShapeDtypeStruct((m, n), out_dtype),
        compiler_params=_cparams(("parallel", "arbitrary")),
        name="matmul",
    )(*args)


def _pack_rows(x):
    half = x.shape[1] // 2
    bits = lambda v: lax.bitcast_convert_type(v.astype(BF16).astype(F32), jnp.uint32)
    return bits(x[:, :half]) | (bits(x[:, half:]) >> 16)


def _unpack_rows(words):
    hi = lax.bitcast_convert_type(words & jnp.uint32(0xFFFF0000), F32)
    lo = lax.bitcast_convert_type(words << 16, F32)
    return hi, lo


def _layer_norm_rows(x, g, b):
    mu = jnp.mean(x, axis=-1, keepdims=True)
    xc = x - mu
    var = jnp.mean(xc * xc, axis=-1, keepdims=True)
    return xc * lax.rsqrt(var + LN_EPS) * g + b


def _ln_router_kernel(h_ref, mix_ref, g_ref, b_ref, rwt_ref, rb_ref,
                      h1_ref, h1p_ref, dest_ref, gate_ref, cnt_ref, carry_ref,
                      *, alpha, n_experts, cap):
    i = pl.program_id(0)
    epg = n_experts // N_GROUPS
    tm = h_ref.shape[0]

    @pl.when(i == 0)
    def _():
        carry_ref[...] = jnp.zeros_like(carry_ref)

    x = alpha * h_ref[...] + mix_ref[...].astype(F32)
    y = _layer_norm_rows(x, g_ref[...], b_ref[...])
    h1_ref[...] = y
    h1p_ref[...] = _pack_rows(y)

    logits = _nt_split(rwt_ref[...], y)
    mx = jnp.max(logits, axis=0, keepdims=True)
    ex = jnp.exp(logits - mx)
    probs = ex / jnp.sum(ex, axis=0, keepdims=True)
    sel = probs + rb_ref[...]

    srow = [sel[e:e + 1, :] for e in range(n_experts)]
    prow = [probs[e:e + 1, :] for e in range(n_experts)]

    def top2_sum(vals):
        best = None
        for a in range(len(vals)):
            for c in range(a + 1, len(vals)):
                s = vals[a] + vals[c]
                best = s if best is None else jnp.maximum(best, s)
        return best

    gscore = [top2_sum(srow[g * epg:(g + 1) * epg]) for g in range(N_GROUPS)]
    best_g = jnp.zeros_like(gscore[0], dtype=jnp.int32)
    best_s = gscore[0]
    for g in range(1, N_GROUPS):
        upd = gscore[g] > best_s
        best_g = jnp.where(upd, g, best_g)
        best_s = jnp.where(upd, gscore[g], best_s)

    def pick(rows, j):
        out = rows[j]
        for g in range(1, N_GROUPS):
            out = jnp.where(best_g == g, rows[g * epg + j], out)
        return out

    in_s = [pick(srow, j) for j in range(epg)]
    in_p = [pick(prow, j) for j in range(epg)]

    def argmax_first(vals, exclude=None):
        bi = None
        bv = None
        for j, v in enumerate(vals):
            if exclude is not None:
                v = jnp.where(exclude == j, -jnp.inf, v)
            if bi is None:
                bi = jnp.zeros_like(best_g)
                bv = v
            else:
                upd = v > bv
                bi = jnp.where(upd, j, bi)
                bv = jnp.where(upd, v, bv)
        return bi

    loc0 = argmax_first(in_s)
    loc1 = argmax_first(in_s, exclude=loc0)

    def take(vals, idx):
        out = vals[0]
        for j in range(1, len(vals)):
            out = jnp.where(idx == j, vals[j], out)
        return out

    p0 = take(in_p, loc0)
    p1 = take(in_p, loc1)
    psum = p0 + p1
    gate_ref[...] = jnp.concatenate([p0 / psum, p1 / psum], axis=0)[None]

    idx0 = best_g * epg + loc0
    idx1 = best_g * epg + loc1

    eid = lax.broadcasted_iota(jnp.int32, (n_experts, tm), 0)
    hit0 = eid == idx0
    hit1 = eid == idx1
    member = jnp.where(hit0, 1.0, jnp.where(hit1, 1.0, 0.0))
    srci = lax.broadcasted_iota(jnp.int32, (tm, tm), 0)
    dsti = lax.broadcasted_iota(jnp.int32, (tm, tm), 1)
    upper = jnp.where(srci < dsti, 1.0, 0.0).astype(BF16)
    prefix = _dot(member.astype(BF16), upper)
    carry = carry_ref[...]
    pos = eid.astype(F32) * float(cap) + carry[:, :1] + prefix
    dest0 = jnp.sum(jnp.where(hit0, pos, 0.0), axis=0, keepdims=True)
    dest1 = jnp.sum(jnp.where(hit1, pos, 0.0), axis=0, keepdims=True)
    dest_ref[...] = jnp.concatenate([dest0, dest1], axis=0).astype(jnp.int32)[None]
    carry = carry + jnp.sum(member, axis=1, keepdims=True)
    carry_ref[...] = carry
    cnt_ref[...] = carry.astype(jnp.int32)


def ln_router(h, mix, ln_g, ln_b, router_wt, router_bias, *, alpha, cap):
    n, d = h.shape
    n_experts = router_wt.shape[0]
    tm = min(TOK_TILE, n)
    nb = n // tm
    row = lambda i: (i, 0)
    const = lambda i: (0, 0)
    return pl.pallas_call(
        functools.partial(_ln_router_kernel, alpha=alpha, n_experts=n_experts, cap=cap),
        grid=(nb,),
        in_specs=[pl.BlockSpec((tm, d), row), pl.BlockSpec((tm, d), row),
                  pl.BlockSpec((1, d), const), pl.BlockSpec((1, d), const),
                  pl.BlockSpec((n_experts, d), const), pl.BlockSpec((n_experts, 1), const)],
        out_specs=[pl.BlockSpec((tm, d), row), pl.BlockSpec((tm, d // 2), row),
                   pl.BlockSpec((1, TOP_K, tm), lambda i: (i, 0, 0)),
                   pl.BlockSpec((1, TOP_K, tm), lambda i: (i, 0, 0)),
                   pl.BlockSpec((n_experts, V7X_LANES), const)],
        out_shape=[jax.ShapeDtypeStruct((n, d), F32), jax.ShapeDtypeStruct((n, d // 2), jnp.uint32),
                   jax.ShapeDtypeStruct((nb, TOP_K, tm), jnp.int32),
                   jax.ShapeDtypeStruct((nb, TOP_K, tm), F32),
                   jax.ShapeDtypeStruct((n_experts, V7X_LANES), jnp.int32)],
        scratch_shapes=[pltpu.VMEM((n_experts, V7X_LANES), F32)],
        compiler_params=_cparams(("arbitrary",)),
        name="ln_router",
    )(h, mix, ln_g.reshape(1, d), ln_b.reshape(1, d), router_wt, router_bias.reshape(n_experts, 1))


def _row_copy(src_ref, src_row, dst_ref, dst_row, sem):
    return pltpu.make_async_copy(src_ref.at[pl.ds(src_row, 1), :], dst_ref.at[pl.ds(dst_row, 1), :], sem)


def _dispatch_kernel(cnt_ref, dest_ref, x_ref, xs_ref, zero_ref, sem, zsem, *, cap, n_experts):
    i = pl.program_id(0)
    tm = x_ref.shape[0]

    for r in range(tm):
        for k in range(TOP_K):
            _row_copy(x_ref, r, xs_ref, dest_ref[k, r], sem).start(priority=k)

    @pl.when(i == pl.num_programs(0) - 1)
    def _():
        zero_ref[...] = jnp.zeros_like(zero_ref)

        def pad_rows(e, do):
            cnt = cnt_ref[e]
            n_pad = lax.rem(EXP_TILE - lax.rem(cnt, EXP_TILE), EXP_TILE)

            def one(r, c):
                do(_row_copy(zero_ref, 0, xs_ref, e * cap + cnt + r, zsem))
                return c

            lax.fori_loop(0, n_pad, one, 0)
            return do

        lax.fori_loop(0, n_experts, lambda e, c: (pad_rows(e, lambda cp: cp.start()), c)[1], 0)
        lax.fori_loop(0, n_experts, lambda e, c: (pad_rows(e, lambda cp: cp.wait()), c)[1], 0)

    def wait(r, c):
        for k in range(TOP_K):
            _row_copy(x_ref, 0, xs_ref, 0, sem).wait()
        return c

    lax.fori_loop(0, tm, wait, 0, unroll=WAIT_UNROLL)


def dispatch(x, dest, counts, *, cap):
    n, d = x.shape
    n_experts = counts.shape[0]
    nb, _, tm = dest.shape
    return pl.pallas_call(
        functools.partial(_dispatch_kernel, cap=cap, n_experts=n_experts),
        grid_spec=pltpu.PrefetchScalarGridSpec(
            num_scalar_prefetch=1,
            grid=(nb,),
            in_specs=[pl.BlockSpec((None, TOP_K, tm), lambda i, c: (i, 0, 0), memory_space=pltpu.SMEM),
                      pl.BlockSpec((tm, d), lambda i, c: (i, 0))],
            out_specs=pl.BlockSpec(memory_space=pl.ANY),
            scratch_shapes=[pltpu.VMEM((V7X_SUBLANES, d), x.dtype),
                            pltpu.SemaphoreType.DMA(()), pltpu.SemaphoreType.DMA(())],
        ),
        out_shape=jax.ShapeDtypeStruct((n_experts * cap, d), x.dtype),
        compiler_params=_cparams(("arbitrary",)),
        name="dispatch",
    )(counts, dest, x)


def _expert_kernel(blk_ref, exp_ref, valid_ref, x_ref, wg_ref, wu_ref, wd_ref, o_ref):
    s = pl.program_id(0)

    @pl.when(valid_ref[s] == 1)
    def _():
        hi, lo = _unpack_rows(x_ref[...])
        x = jnp.concatenate([hi.astype(BF16), lo.astype(BF16)], axis=1)
        gate = _dot(x, wg_ref[...])
        up = _dot(x, wu_ref[...])
        act = (jax.nn.silu(gate) * up).astype(BF16)
        o_ref[...] = _pack_rows(_dot(act, wd_ref[...]))


def expert_ffn(xs, w_gate, w_up, w_down, layer, counts, *, cap, n_tokens):
    d = xs.shape[1]
    n_experts, f = w_gate.shape[1], w_gate.shape[3]
    tm = EXP_TILE
    n_tiles = (TOP_K * n_tokens) // tm + n_experts
    tiles_e = (counts + tm - 1) // tm
    cum = jnp.cumsum(tiles_e)
    total = cum[-1]
    slot = jnp.minimum(jnp.arange(n_tiles, dtype=jnp.int32), total - 1)
    e_of = jnp.sum((slot[:, None] >= cum[None, :]).astype(jnp.int32), axis=1)
    first = jnp.take(cum - tiles_e, e_of)
    blk = (e_of * (cap // tm) + slot - first).astype(jnp.int32)
    valid = (jnp.arange(n_tiles) < total).astype(jnp.int32)
    xmap = lambda s, blk, ex, va: (blk[s], 0)
    return pl.pallas_call(
        _expert_kernel,
        grid_spec=pltpu.PrefetchScalarGridSpec(
            num_scalar_prefetch=3,
            grid=(n_tiles,),
            in_specs=[pl.BlockSpec((tm, d), xmap),
                      pl.BlockSpec((None, None, 2 * d, f), lambda s, blk, ex, va: (layer, ex[s], 0, 0)),
                      pl.BlockSpec((None, None, 2 * d, f), lambda s, blk, ex, va: (layer, ex[s], 0, 0)),
                      pl.BlockSpec((None, None, f, 2 * d), lambda s, blk, ex, va: (layer, ex[s], 0, 0))],
            out_specs=pl.BlockSpec((tm, d), xmap),
        ),
        out_shape=jax.ShapeDtypeStruct(xs.shape, jnp.uint32),
        compiler_params=_cparams(("arbitrary",)),
        name="expert_ffn",
    )(blk, e_of.astype(jnp.int32), valid, xs, w_gate, w_up, w_down)


def _combine_kernel(dest_ref, next_dest_ref, h_ref, gate_ref, g_ref, b_ref, ys_ref, o_ref, ob_ref, buf_ref, sems,
                    *, alpha):
    i = pl.program_id(0)
    last = pl.num_programs(0) - 1
    tm = h_ref.shape[0]
    parity = lax.rem(i, 2)

    def gather(idx_ref, to_slot):
        for r in range(tm):
            for k in range(TOP_K):
                _row_copy(ys_ref, idx_ref[k, r], buf_ref.at[to_slot, k], r, sems.at[to_slot]).start(priority=k)

    def wait_slot(slot):
        def wait(r, c):
            for k in range(TOP_K):
                _row_copy(ys_ref, 0, buf_ref.at[slot, k], 0, sems.at[slot]).wait()
            return c

        lax.fori_loop(0, tm, wait, 0, unroll=WAIT_UNROLL)

    @pl.when(i == 0)
    def _():
        gather(dest_ref, 0)

    def step(cur):
        wait_slot(cur)
        gather(next_dest_ref, 1 - cur)
        gates = gate_ref[...]
        hi0, lo0 = _unpack_rows(buf_ref[cur, 0])
        hi1, lo1 = _unpack_rows(buf_ref[cur, 1])
        ffn = jnp.concatenate([gates[:, 0:1] * hi0 + gates[:, 1:2] * hi1,
                               gates[:, 0:1] * lo0 + gates[:, 1:2] * lo1], axis=1)
        x = alpha * h_ref[...] + ffn
        y = _layer_norm_rows(x, g_ref[...], b_ref[...])
        o_ref[...] = y
        ob_ref[...] = y.astype(BF16)

        @pl.when(i == last)
        def _():
            wait_slot(1 - cur)

    for cur in range(2):
        pl.when(parity == cur)(functools.partial(step, cur))


def combine(h, dest, gates_cols, ln_g, ln_b, ys, *, alpha):
    n, d = h.shape
    nb, _, tm = dest.shape
    row = lambda i: (i, 0)
    const = lambda i: (0, 0)
    return pl.pallas_call(
        functools.partial(_combine_kernel, alpha=alpha),
        grid=(nb,),
        in_specs=[pl.BlockSpec((None, TOP_K, tm), lambda i: (i, 0, 0), memory_space=pltpu.SMEM),
                  pl.BlockSpec((None, TOP_K, tm), lambda i: (jnp.minimum(i + 1, nb - 1), 0, 0),
                               memory_space=pltpu.SMEM),
                  pl.BlockSpec((tm, d), row), pl.BlockSpec((tm, TOP_K), row),
                  pl.BlockSpec((1, d), const), pl.BlockSpec((1, d), const),
                  pl.BlockSpec(memory_space=pl.ANY)],
        out_specs=[pl.BlockSpec((tm, d), row), pl.BlockSpec((tm, d), row)],
        out_shape=[jax.ShapeDtypeStruct((n, d), F32), jax.ShapeDtypeStruct((n, d), BF16)],
        scratch_shapes=[pltpu.VMEM((2, TOP_K, tm, d // 2), jnp.uint32), pltpu.SemaphoreType.DMA((2,))],
        compiler_params=_cparams(("arbitrary",)),
        name="combine",
    )(dest, dest, h, gates_cols, ln_g.reshape(1, d), ln_b.reshape(1, d), ys)


def moe_block(h, mix, ln_g, ln_b, router_wt, router_bias, w_gate, w_up, w_down, layer, *, alpha):
    n, d = h.shape
    cap = n + EXP_TILE
    h1, h1p, dest, gates, cnt = ln_router(h, mix, ln_g[0], ln_b[0], router_wt, router_bias,
                                          alpha=alpha, cap=cap)
    counts = cnt[:, 0]
    xs = dispatch(h1p, dest, counts, cap=cap)
    ys = expert_ffn(xs, w_gate, w_up, w_down, layer, counts, cap=cap, n_tokens=n)
    gates_cols = jnp.transpose(gates, (0, 2, 1)).reshape(n, TOP_K)
    return combine(h1, dest, gates_cols, ln_g[1], ln_b[1], ys, alpha=alpha)


def _chunk_masks(n, chunk):
    ri = lax.broadcasted_iota(jnp.int32, (n, n), 0)
    ci = lax.broadcasted_iota(jnp.int32, (n, n), 1)
    same = (ri // chunk) == (ci // chunk)
    return same & (ri >= ci), same & (ri > ci)


def _gate_params_kernel(h_ref, w_ref, alog_ref, dtb_ref, o_ref, *, n_heads):
    tm = h_ref.shape[0]
    h_hi, h_lo = _split_bf16(h_ref[...])
    w_hi, w_lo = _split_bf16(w_ref[...])
    ab = _dot(h_hi, w_hi) + (_dot(h_hi, w_lo) + _dot(h_lo, w_hi))
    g = -jnp.exp(alog_ref[...]) * jax.nn.softplus(ab + dtb_ref[...])
    beta = jax.nn.sigmoid(ab)
    causal, _ = _chunk_masks(tm, DN_CHUNK)
    tri = jnp.where(causal, 1.0, 0.0).astype(BF16)
    g_hi, g_rest = _split_bf16(g)
    g_mid, g_lo = _split_bf16(g - g_hi.astype(F32))
    del g_rest
    gc = _dot(tri, g_hi) + (_dot(tri, g_mid) + _dot(tri, g_lo))
    lane = lax.broadcasted_iota(jnp.int32, ab.shape, 1)
    o_ref[...] = jnp.where(lane < n_heads, gc, beta)


def gate_params(h, w_ab, a_log, dt_bias):
    n, d = h.shape
    n_heads = a_log.shape[0]
    tm = min(AB_TILE, n)
    pad = V7X_LANES - 2 * n_heads
    w = jnp.pad(w_ab, ((0, 0), (0, pad)))
    lanes = lambda v: jnp.pad(v.astype(F32), (0, V7X_LANES - n_heads)).reshape(1, V7X_LANES)
    const = lambda i: (0, 0)
    return pl.pallas_call(
        functools.partial(_gate_params_kernel, n_heads=n_heads),
        grid=(n // tm,),
        in_specs=[pl.BlockSpec((tm, d), lambda i: (i, 0)), pl.BlockSpec((d, V7X_LANES), const),
                  pl.BlockSpec((1, V7X_LANES), const), pl.BlockSpec((1, V7X_LANES), const)],
        out_specs=pl.BlockSpec((tm, V7X_LANES), lambda i: (i, 0)),
        out_shape=jax.ShapeDtypeStruct((n, V7X_LANES), F32),
        compiler_params=_cparams(("parallel",)),
        name="gate_params",
    )(h, w, lanes(a_log), lanes(dt_bias))


def _unit_lower_inverses(mats):
    n = mats[0].shape[0]
    ri = lax.broadcasted_iota(jnp.int32, (n, n), 0)
    ci = lax.broadcasted_iota(jnp.int32, (n, n), 1)
    differ = ri ^ ci
    eye = jnp.where(ri == ci, 1.0, 0.0)
    base = (differ >> 1) == 0
    invs = [eye - jnp.where(base, a, 0.0) for a in mats]
    level = 1
    while (1 << level) < DN_CHUNK:
        at_level = (differ >> level) == 1
        invbs = [inv.astype(BF16) for inv in invs]
        halves = [_dot(jnp.where(at_level, a, 0.0).astype(BF16), invb).astype(BF16)
                  for a, invb in zip(mats, invbs)]
        invs = [inv - _dot(invb, half) for inv, invb, half in zip(invs, invbs, halves)]
        level += 1
    return invs


def _deltanet_kernel(q_ref, k_ref, v_ref, z_ref, cq_ref, ck_ref, cv_ref, gb_ref, gt_ref, ng_ref, y_any,
                     o_ref, s_ref, carry_ref, *, hb, dk):
    del y_any
    t = pl.program_id(2)
    tb = q_ref.shape[0]
    scale = dk ** -0.5

    @pl.when(t == 0)
    def _():
        s_ref[...] = jnp.zeros_like(s_ref)
        carry_ref[...] = jnp.zeros_like(carry_ref)

    def conv_silu(x_ref, w_ref, slot):
        raw = x_ref[...].astype(F32)
        ext = jnp.concatenate([carry_ref[slot], raw], axis=0)
        w = w_ref[...]
        taps = w.shape[0]
        acc = None
        for j in range(taps):
            lag = taps - 1 - j
            window = pltpu.roll(ext, lag, axis=0)[V7X_SUBLANES:] if lag else raw
            term = window * w[j:j + 1, :]
            acc = term if acc is None else acc + term
        carry_ref[slot] = raw[tb - V7X_SUBLANES:tb]
        return jax.nn.silu(acc)

    qa = conv_silu(q_ref, cq_ref, 0)
    ka = conv_silu(k_ref, ck_ref, 1)
    va = conv_silu(v_ref, cv_ref, 2)
    za = z_ref[...].astype(F32)
    gbl = gb_ref[...]
    gt = gt_ref[...]
    sr = min(DN_SOLVE, tb)
    n_sub = tb // sr
    causal, strict = _chunk_masks(sr, DN_CHUNK)

    heads = range(hb)
    n_chunks = tb // DN_CHUNK
    qn, kn, gc_col, a, rhs, qk = [], [], [], [], [], []
    for hh in heads:
        sl = slice(hh * dk, (hh + 1) * dk)
        q, k, v = qa[:, sl], ka[:, sl], va[:, sl]
        qn.append(q * lax.rsqrt(jnp.sum(q * q, axis=-1, keepdims=True) + RMS_EPS) * scale)
        kn.append(k * lax.rsqrt(jnp.sum(k * k, axis=-1, keepdims=True) + RMS_EPS))
        gc_col.append(gbl[:, hh:hh + 1])
        beta = gbl[:, hb + hh:hb + hh + 1]
        kb = kn[hh] * beta
        rhs.append(jnp.concatenate([v * beta, kb * jnp.exp(gc_col[hh])], axis=1).astype(BF16))
        for p in range(n_sub):
            rows = slice(p * sr, (p + 1) * sr)
            gc_row = gt[hh:hh + 1, rows]
            decay = jnp.where(causal, jnp.exp(jnp.where(causal, gc_col[hh][rows] - gc_row, 0.0)), 0.0)
            knb = kn[hh][rows].astype(BF16)
            a.append(jnp.where(strict, _nt(kb[rows].astype(BF16), knb) * decay, 0.0))
            qk.append(jnp.where(causal, _nt(qn[hh][rows].astype(BF16), knb) * decay, 0.0).astype(BF16))
    tinv = _unit_lower_inverses(a)
    solb = [jnp.concatenate([_dot(tinv[hh * n_sub + p].astype(BF16), rhs[hh][p * sr:(p + 1) * sr]).astype(BF16)
                             for p in range(n_sub)], axis=0) for hh in heads]
    qk_sol = [jnp.concatenate([_dot(qk[hh * n_sub + p], solb[hh][p * sr:(p + 1) * sr]) for p in range(n_sub)],
                              axis=0) for hh in heads]
    q_eff = [(qn[hh] * jnp.exp(gc_col[hh]) - qk_sol[hh][:, dk:]).astype(BF16) for hh in heads]
    g_last = [[gc_col[hh][(c + 1) * DN_CHUNK - 1:(c + 1) * DN_CHUNK, :] for c in range(n_chunks)] for hh in heads]
    kd_sol = [[_tn((kn[hh][c * DN_CHUNK:(c + 1) * DN_CHUNK]
                    * jnp.exp(g_last[hh][c] - gc_col[hh][c * DN_CHUNK:(c + 1) * DN_CHUNK])).astype(BF16),
                   solb[hh][c * DN_CHUNK:(c + 1) * DN_CHUNK])
               for c in range(n_chunks)] for hh in heads]
    s = [s_ref[hh] for hh in heads]
    outs = [[] for _ in heads]
    for c in range(n_chunks):
        lo, hi = c * DN_CHUNK, (c + 1) * DN_CHUNK
        for hh in heads:
            sb = s[hh].astype(BF16)
            outs[hh].append(qk_sol[hh][lo:hi, :dk] + _dot(q_eff[hh][lo:hi], sb))
            s[hh] = (s[hh] * jnp.exp(g_last[hh][c]) - _dot(kd_sol[hh][c][:, dk:].astype(BF16), sb)
                     + kd_sol[hh][c][:, :dk])
    for hh in heads:
        s_ref[hh] = s[hh]
        o = jnp.concatenate(outs[hh], axis=0)
        o = o * lax.rsqrt(jnp.mean(o * o, axis=-1, keepdims=True) + RMS_EPS) * ng_ref[...]
        o = o * jax.nn.silu(za[:, hh * dk:(hh + 1) * dk])
        o_ref[:, hh * dk:(hh + 1) * dk] = o.astype(o_ref.dtype)


def deltanet(qkvz, conv_w, gparams, norm_g, y, *, batch, n_heads):
    n = qkvz.shape[0]
    dk = norm_g.shape[0]
    hb = min(DN_HB, n_heads)
    w = hb * dk
    nhb = n_heads // hb
    seq = n // batch
    tb = min(DN_TB, seq)
    nt = seq // tb
    gc = gparams[:, :n_heads].reshape(n, nhb, hb)
    beta = gparams[:, n_heads:2 * n_heads].reshape(n, nhb, hb)
    cols = jnp.concatenate([gc, beta], axis=-1).transpose(1, 0, 2)
    cols = jnp.pad(cols, ((0, 0), (0, 0), (0, V7X_LANES - 2 * hb)))
    rows = jnp.pad(gc.transpose(1, 2, 0), ((0, 0), (0, V7X_SUBLANES - hb), (0, 0)))
    rmap = lambda part: (lambda b, h, t: (b * nt + t, part * nhb + h))
    cmap = lambda part: (lambda b, h, t: (0, part * nhb + h))
    taps = conv_w.shape[0]
    return pl.pallas_call(
        functools.partial(_deltanet_kernel, hb=hb, dk=dk),
        grid=(batch, nhb, nt),
        in_specs=[pl.BlockSpec((tb, w), rmap(0)), pl.BlockSpec((tb, w), rmap(1)),
                  pl.BlockSpec((tb, w), rmap(2)), pl.BlockSpec((tb, w), rmap(3)),
                  pl.BlockSpec((taps, w), cmap(0)), pl.BlockSpec((taps, w), cmap(1)),
                  pl.BlockSpec((taps, w), cmap(2)),
                  pl.BlockSpec((None, tb, V7X_LANES), lambda b, h, t: (h, b * nt + t, 0)),
                  pl.BlockSpec((None, V7X_SUBLANES, tb), lambda b, h, t: (h, 0, b * nt + t)),
                  pl.BlockSpec((1, dk), lambda b, h, t: (0, 0)),
                  pl.BlockSpec(memory_space=pl.ANY)],
        out_specs=pl.BlockSpec((tb, w), rmap(0)),
        out_shape=jax.ShapeDtypeStruct(y.shape, y.dtype),
        scratch_shapes=[pltpu.VMEM((hb, dk, dk), F32), pltpu.VMEM((3, V7X_SUBLANES, w), F32)],
        input_output_aliases={10: 0},
        compiler_params=_cparams(("parallel", "parallel", "arbitrary")),
        name="deltanet",
    )(qkvz, qkvz, qkvz, qkvz, conv_w, conv_w, conv_w, cols, rows, norm_g.reshape(1, dk), y)


POOL_HALO = 16


def _pool_kernel(x_ref, w_ref, b_ref, sc_ref, o_ref, carry_ref):
    t = pl.program_id(1)
    tt = x_ref.shape[0]
    cg = w_ref.shape[1]

    @pl.when(t == 0)
    def _():
        carry_ref[...] = jnp.zeros_like(carry_ref)

    x = x_ref[...].astype(F32)
    ext = jnp.concatenate([carry_ref[...], x], axis=0)
    carry_ref[...] = x[tt - POOL_HALO:tt]
    pos = t * tt + lax.broadcasted_iota(jnp.int32, (tt, 1), 0)
    for gi, win in enumerate(POOL_WINDOWS):
        cols = slice(gi * cg, (gi + 1) * cg)
        cur = ext[:, cols]
        span = 1
        while span < win:
            cur = cur[span:] + cur[:cur.shape[0] - span]
            span *= 2
        lo = POOL_HALO - (win - 1)
        total = cur[lo:lo + tt]
        count = jnp.minimum(pos + 1, win).astype(F32)
        pooled = total / count - x[:, cols]
        mixed = _dot(pooled.astype(BF16), w_ref[gi]) + b_ref[:, cols]
        o_ref[:, cols] = (mixed * sc_ref[:, cols]).astype(o_ref.dtype)


def pool(xp, pool_w, pool_b, pool_scale, *, batch, out_width):
    n, p = xp.shape
    assert out_width % p == 0 and max(POOL_WINDOWS) <= POOL_HALO
    seq = n // batch
    tt = min(POOL_TILE, seq)
    nt = seq // tt
    g, cg, _ = pool_w.shape
    const = lambda b, t: (0, 0)
    return pl.pallas_call(
        _pool_kernel,
        grid=(batch, nt),
        in_specs=[pl.BlockSpec((tt, p), lambda b, t: (b * nt + t, 0)),
                  pl.BlockSpec((g, cg, cg), lambda b, t: (0, 0, 0)),
                  pl.BlockSpec((1, p), const), pl.BlockSpec((1, p), const)],
        out_specs=pl.BlockSpec((tt, p), lambda b, t: (b * nt + t, out_width // p - 1)),
        out_shape=jax.ShapeDtypeStruct((n, out_width), BF16),
        scratch_shapes=[pltpu.VMEM((POOL_HALO, p), F32)],
        compiler_params=_cparams(("parallel", "arbitrary")),
        name="pool",
    )(xp, pool_w, pool_b.reshape(1, p).astype(F32), pool_scale.reshape(1, p).astype(F32))


def _sgu_kernel(u_ref, v_ref, g_ref, b_ref, ws_ref, bs_ref, o_ref):
    tt, width = u_ref.shape
    n_heads, chunk, _ = ws_ref.shape
    hd = width // n_heads
    vn = _layer_norm_rows(v_ref[...].astype(F32), g_ref[...], b_ref[...]).astype(BF16)
    ri = lax.broadcasted_iota(jnp.int32, (chunk, chunk), 0)
    ci = lax.broadcasted_iota(jnp.int32, (chunk, chunk), 1)
    for g in range(n_heads):
        cols = slice(g * hd, (g + 1) * hd)
        wsg = jnp.where(ri >= ci, ws_ref[g], 0.0).astype(BF16)
        bias = bs_ref[:, g:g + 1]
        for c in range(tt // chunk):
            rows = slice(c * chunk, (c + 1) * chunk)
            mixed = _dot(wsg, vn[rows, cols]) + bias
            o_ref[rows, cols] = (u_ref[rows, cols].astype(F32) * mixed).astype(o_ref.dtype)


def sgu(uv, ln_g, ln_b, w_s, b_s):
    n, two_w = uv.shape
    width = two_w // 2
    n_heads, chunk, _ = w_s.shape
    tt = min(SGU_TILE, n)
    assert tt % chunk == 0
    const = lambda i: (0, 0)
    return pl.pallas_call(
        _sgu_kernel,
        grid=(n // tt,),
        in_specs=[pl.BlockSpec((tt, width), lambda i: (i, 0)), pl.BlockSpec((tt, width), lambda i: (i, 1)),
                  pl.BlockSpec((1, width), const), pl.BlockSpec((1, width), const),
                  pl.BlockSpec((n_heads, chunk, chunk), lambda i: (0, 0, 0)),
                  pl.BlockSpec((chunk, n_heads), const)],
        out_specs=pl.BlockSpec((tt, width), lambda i: (i, 0)),
        out_shape=jax.ShapeDtypeStruct((n, width), BF16),
        compiler_params=_cparams(("parallel",)),
        name="sgu",
    )(uv, uv, ln_g.reshape(1, width), ln_b.reshape(1, width), w_s, jnp.transpose(b_s))


def even_mixer(hf, hb16, w_in, w_conv, a_log, dt_bias, norm_g, pool_w, pool_b, pool_scale, w_out, *, batch):
    n_heads = a_log.shape[0]
    dk = norm_g.shape[0]
    dn_width = n_heads * dk
    o0 = 4 * dn_width
    o1 = o0 + 2 * n_heads
    qkvz = matmul(hb16, w_in[:, :o0].astype(BF16))
    xp = matmul(hb16, w_in[:, o1:].astype(BF16))
    gparams = gate_params(hf, w_in[:, o0:o1], a_log, dt_bias)
    y = pool(xp, pool_w.astype(BF16), pool_b, pool_scale, batch=batch, out_width=w_out.shape[0])
    y = deltanet(qkvz, w_conv, gparams, norm_g, y, batch=batch, n_heads=n_heads)
    return matmul(y, w_out.astype(BF16))


def odd_mixer(hb16, w_in, b_in, ln_g, ln_b, w_s, b_s, w_out):
    uv = matmul(hb16, w_in.astype(BF16), bias=b_in, act="gelu")
    gated = sgu(uv, ln_g, ln_b, w_s, b_s)
    return matmul(gated, w_out.astype(BF16))


def kernel(x, ev_w_in, ev_conv, ev_a_log, ev_dt_bias, ev_norm_g, ev_pool_w, ev_pool_b, ev_pool_scale, ev_w_out, od_w_in, od_b_in, od_ln_g, od_ln_b, od_w_s, od_b_s, od_w_out, ln_g, ln_b, router_w, router_bias, moe_w_gate, moe_w_up, moe_w_down):
    batch, seq, d = x.shape
    depth = ln_g.shape[0]
    alpha = (2.0 * depth) ** 0.25
    hf = x.reshape(batch * seq, d)
    hb16 = hf.astype(BF16)
    router_wt = jnp.transpose(router_w)
    wg, wu, wd = moe_w_gate.astype(BF16), moe_w_up.astype(BF16), moe_w_down.astype(BF16)
    for layer in range(depth):
        i = layer // 2
        if layer % 2 == 0:
            mix = even_mixer(hf, hb16, ev_w_in[i], ev_conv[i], ev_a_log[i], ev_dt_bias[i], ev_norm_g[i],
                             ev_pool_w[i], ev_pool_b[i], ev_pool_scale[i], ev_w_out[i], batch=batch)
        else:
            mix = odd_mixer(hb16, od_w_in[i], od_b_in[i], od_ln_g[i], od_ln_b[i], od_w_s[i], od_b_s[i],
                            od_w_out[i])
        hf, hb16 = moe_block(hf, mix, ln_g[layer], ln_b[layer], router_wt, router_bias, wg, wu, wd, layer,
                             alpha=alpha)
    return hf.reshape(batch, seq, d)
```

```python
import functools

import jax
import jax.numpy as jnp
from jax import lax
from jax.experimental import pallas as pl
from jax.experimental.pallas import tpu as pltpu

F32 = jnp.float32
BF16 = jnp.bfloat16

V7X_LANES = 128
V7X_SUBLANES = 8
V7X_VMEM_BYTES = 64 * 1024 * 1024
VMEM_LIMIT = V7X_VMEM_BYTES - 8 * 1024 * 1024

DN_CHUNK = 64
POOL_WINDOWS = (2, 4, 8, 16)
N_GROUPS = 4
TOP_K = 2
LN_EPS = 1e-5
RMS_EPS = 1e-6

MM_TM = 1024
MM_TN = 1024
TOK_TILE = 256
EXP_TILE = 512
DN_TB = 256
DN_SOLVE = 128
DN_HB = 8
WAIT_UNROLL = 16
AB_TILE = 256
POOL_TILE = 256
SGU_TILE = 256


def _cparams(sem):
    return pltpu.CompilerParams(dimension_semantics=sem, vmem_limit_bytes=VMEM_LIMIT)


def _nt(a, b):
    return lax.dot_general(a, b, (((1,), (1,)), ((), ())), preferred_element_type=F32)


def _tn(a, b):
    return lax.dot_general(a, b, (((0,), (0,)), ((), ())), preferred_element_type=F32)


def _dot(a, b):
    return jnp.dot(a, b, preferred_element_type=F32)


def _split_bf16(x):
    hi = x.astype(BF16)
    return hi, (x - hi.astype(F32)).astype(BF16)


def _nt_split(a, b):
    a_hi, a_lo = _split_bf16(a)
    b_hi, b_lo = _split_bf16(b)
    return _nt(a_hi, b_hi) + (_nt(a_hi, b_lo) + _nt(a_lo, b_hi))


def _mm_kernel(*refs, act, has_bias, n_riders):
    n_in = 3 if has_bias else 2
    x_ref, w_ref = refs[0], refs[1]
    rider_in = refs[n_in:n_in + n_riders]
    o_ref = refs[n_in + n_riders]
    rider_out = refs[n_in + n_riders + 1:]
    acc = _dot(x_ref[...], w_ref[...])
    if has_bias:
        acc = acc + refs[2][...]
    if act == "gelu":
        acc = jax.nn.gelu(acc)
    o_ref[...] = acc.astype(o_ref.dtype)
    for src, dst in zip(rider_in, rider_out):
        dst[...] = src[...].astype(dst.dtype)


def matmul(x, w, bias=None, act=None, out_dtype=BF16, riders=(), n_cols=None):
    m, k = x.shape
    n = w.shape[1] if n_cols is None else n_cols
    tm, tn = min(MM_TM, m), min(MM_TN, n)
    assert m % tm == 0 and n % tn == 0
    ni, nj = m // tm, n // tn
    in_specs = [pl.BlockSpec((tm, k), lambda i, j: (i, 0)),
                pl.BlockSpec((k, tn), lambda i, j: (0, j))]
    args = [x, w]
    if bias is not None:
        in_specs.append(pl.BlockSpec((1, tn), lambda i, j: (0, j)))
        args.append(bias.reshape(1, n).astype(F32))
    out_specs = [pl.BlockSpec((tm, tn), lambda i, j: (i, j))]
    out_shape = [jax.ShapeDtypeStruct((m, n), out_dtype)]
    for arr, rows, layer in riders:
        cols = arr.shape[1]
        rb = rows // (ni * nj)
        assert rb * ni * nj == rows and rb % (2 * V7X_SUBLANES) == 0
        in_specs.append(pl.BlockSpec((rb, cols), lambda i, j, layer=layer: (layer * ni * nj + i * nj + j, 0)))
        args.append(arr)
        out_specs.append(pl.BlockSpec((rb, cols), lambda i, j: (i * nj + j, 0)))
        out_shape.append(jax.ShapeDtypeStruct((rows, cols), BF16))
    outs = pl.pallas_call(
        functools.partial(_mm_kernel, act=act, has_bias=bias is not None, n_riders=len(riders)),
        grid=(ni, nj),
        in_specs=in_specs,
        out_specs=out_specs,
        out_shape=out_shape,
        compiler_params=_cparams(("parallel", "arbitrary")),
        name="matmul",
    )(*args)
    return outs[0], list(outs[1:])


def _pack_rows(x):
    half = x.shape[1] // 2
    bits = lambda v: lax.bitcast_convert_type(v.astype(BF16).astype(F32), jnp.uint32)
    return bits(x[:, :half]) | (bits(x[:, half:]) >> 16)


def _unpack_rows(words):
    hi = lax.bitcast_convert_type(words & jnp.uint32(0xFFFF0000), F32)
    lo = lax.bitcast_convert_type(words << 16, F32)
    return hi, lo


def _layer_norm_rows(x, g, b):
    mu = jnp.mean(x, axis=-1, keepdims=True)
    xc = x - mu
    var = jnp.mean(xc * xc, axis=-1, keepdims=True)
    return xc * lax.rsqrt(var + LN_EPS) * g + b


def _ln_router_kernel(h_ref, mix_ref, g_ref, b_ref, rwt_ref, rb_ref,
                      h1_ref, h1p_ref, dest_ref, gate_ref, cnt_ref, carry_ref,
                      *, alpha, n_experts, cap):
    i = pl.program_id(0)
    epg = n_experts // N_GROUPS
    tm = h_ref.shape[0]

    @pl.when(i == 0)
    def _():
        carry_ref[...] = jnp.zeros_like(carry_ref)

    x = alpha * h_ref[...] + mix_ref[...].astype(F32)
    y = _layer_norm_rows(x, g_ref[...], b_ref[...])
    h1_ref[...] = y
    h1p_ref[...] = _pack_rows(y)

    logits = _nt_split(rwt_ref[...], y)
    mx = jnp.max(logits, axis=0, keepdims=True)
    ex = jnp.exp(logits - mx)
    probs = ex / jnp.sum(ex, axis=0, keepdims=True)
    sel = probs + rb_ref[...]

    srow = [sel[e:e + 1, :] for e in range(n_experts)]
    prow = [probs[e:e + 1, :] for e in range(n_experts)]

    def top2_sum(vals):
        best = None
        for a in range(len(vals)):
            for c in range(a + 1, len(vals)):
                s = vals[a] + vals[c]
                best = s if best is None else jnp.maximum(best, s)
        return best

    gscore = [top2_sum(srow[g * epg:(g + 1) * epg]) for g in range(N_GROUPS)]
    best_g = jnp.zeros_like(gscore[0], dtype=jnp.int32)
    best_s = gscore[0]
    for g in range(1, N_GROUPS):
        upd = gscore[g] > best_s
        best_g = jnp.where(upd, g, best_g)
        best_s = jnp.where(upd, gscore[g], best_s)

    def pick(rows, j):
        out = rows[j]
        for g in range(1, N_GROUPS):
            out = jnp.where(best_g == g, rows[g * epg + j], out)
        return out

    in_s = [pick(srow, j) for j in range(epg)]
    in_p = [pick(prow, j) for j in range(epg)]

    def argmax_first(vals, exclude=None):
        bi = None
        bv = None
        for j, v in enumerate(vals):
            if exclude is not None:
                v = jnp.where(exclude == j, -jnp.inf, v)
            if bi is None:
                bi = jnp.zeros_like(best_g)
                bv = v
            else:
                upd = v > bv
                bi = jnp.where(upd, j, bi)
                bv = jnp.where(upd, v, bv)
        return bi

    loc0 = argmax_first(in_s)
    loc1 = argmax_first(in_s, exclude=loc0)

    def take(vals, idx):
        out = vals[0]
        for j in range(1, len(vals)):
            out = jnp.where(idx == j, vals[j], out)
        return out

    p0 = take(in_p, loc0)
    p1 = take(in_p, loc1)
    psum = p0 + p1
    gate_ref[...] = jnp.concatenate([p0 / psum, p1 / psum], axis=0)[None]

    idx0 = best_g * epg + loc0
    idx1 = best_g * epg + loc1

    eid = lax.broadcasted_iota(jnp.int32, (n_experts, tm), 0)
    hit0 = eid == idx0
    hit1 = eid == idx1
    member = jnp.where(hit0, 1.0, jnp.where(hit1, 1.0, 0.0))
    srci = lax.broadcasted_iota(jnp.int32, (tm, tm), 0)
    dsti = lax.broadcasted_iota(jnp.int32, (tm, tm), 1)
    upper = jnp.where(srci < dsti, 1.0, 0.0).astype(BF16)
    prefix = _dot(member.astype(BF16), upper)
    carry = carry_ref[...]
    pos = eid.astype(F32) * float(cap) + carry[:, :1] + prefix
    dest0 = jnp.sum(jnp.where(hit0, pos, 0.0), axis=0, keepdims=True)
    dest1 = jnp.sum(jnp.where(hit1, pos, 0.0), axis=0, keepdims=True)
    dest_ref[...] = jnp.concatenate([dest0, dest1], axis=0).astype(jnp.int32)[None]
    carry = carry + jnp.sum(member, axis=1, keepdims=True)
    carry_ref[...] = carry
    cnt_ref[...] = carry.astype(jnp.int32)


def ln_router(h, mix, ln_g, ln_b, router_wt, router_bias, *, alpha, cap):
    n, d = h.shape
    n_experts = router_wt.shape[0]
    tm = min(TOK_TILE, n)
    nb = n // tm
    row = lambda i: (i, 0)
    const = lambda i: (0, 0)
    return pl.pallas_call(
        functools.partial(_ln_router_kernel, alpha=alpha, n_experts=n_experts, cap=cap),
        grid=(nb,),
        in_specs=[pl.BlockSpec((tm, d), row), pl.BlockSpec((tm, d), row),
                  pl.BlockSpec((1, d), const), pl.BlockSpec((1, d), const),
                  pl.BlockSpec((n_experts, d), const), pl.BlockSpec((n_experts, 1), const)],
        out_specs=[pl.BlockSpec((tm, d), row), pl.BlockSpec((tm, d // 2), row),
                   pl.BlockSpec((1, TOP_K, tm), lambda i: (i, 0, 0)),
                   pl.BlockSpec((1, TOP_K, tm), lambda i: (i, 0, 0)),
                   pl.BlockSpec((n_experts, V7X_LANES), const)],
        out_shape=[jax.ShapeDtypeStruct((n, d), F32), jax.ShapeDtypeStruct((n, d // 2), jnp.uint32),
                   jax.ShapeDtypeStruct((nb, TOP_K, tm), jnp.int32),
                   jax.ShapeDtypeStruct((nb, TOP_K, tm), F32),
                   jax.ShapeDtypeStruct((n_experts, V7X_LANES), jnp.int32)],
        scratch_shapes=[pltpu.VMEM((n_experts, V7X_LANES), F32)],
        compiler_params=_cparams(("arbitrary",)),
        name="ln_router",
    )(h, mix, ln_g.reshape(1, d), ln_b.reshape(1, d), router_wt, router_bias.reshape(n_experts, 1))


def _row_copy(src_ref, src_row, dst_ref, dst_row, sem):
    return pltpu.make_async_copy(src_ref.at[pl.ds(src_row, 1), :], dst_ref.at[pl.ds(dst_row, 1), :], sem)


def _dispatch_kernel(cnt_ref, dest_ref, x_ref, xs_ref, zero_ref, sem, zsem, *, cap, n_experts):
    i = pl.program_id(0)
    tm = x_ref.shape[0]

    for r in range(tm):
        for k in range(TOP_K):
            _row_copy(x_ref, r, xs_ref, dest_ref[k, r], sem).start(priority=k)

    @pl.when(i == pl.num_programs(0) - 1)
    def _():
        zero_ref[...] = jnp.zeros_like(zero_ref)

        def pad_rows(e, do):
            cnt = cnt_ref[e]
            n_pad = lax.rem(EXP_TILE - lax.rem(cnt, EXP_TILE), EXP_TILE)

            def one(r, c):
                do(_row_copy(zero_ref, 0, xs_ref, e * cap + cnt + r, zsem))
                return c

            lax.fori_loop(0, n_pad, one, 0)
            return do

        lax.fori_loop(0, n_experts, lambda e, c: (pad_rows(e, lambda cp: cp.start()), c)[1], 0)
        lax.fori_loop(0, n_experts, lambda e, c: (pad_rows(e, lambda cp: cp.wait()), c)[1], 0)

    def wait(r, c):
        for k in range(TOP_K):
            _row_copy(x_ref, 0, xs_ref, 0, sem).wait()
        return c

    lax.fori_loop(0, tm, wait, 0, unroll=WAIT_UNROLL)


def dispatch(x, dest, counts, *, cap):
    n, d = x.shape
    n_experts = counts.shape[0]
    nb, _, tm = dest.shape
    return pl.pallas_call(
        functools.partial(_dispatch_kernel, cap=cap, n_experts=n_experts),
        grid_spec=pltpu.PrefetchScalarGridSpec(
            num_scalar_prefetch=1,
            grid=(nb,),
            in_specs=[pl.BlockSpec((None, TOP_K, tm), lambda i, c: (i, 0, 0), memory_space=pltpu.SMEM),
                      pl.BlockSpec((tm, d), lambda i, c: (i, 0))],
            out_specs=pl.BlockSpec(memory_space=pl.ANY),
            scratch_shapes=[pltpu.VMEM((V7X_SUBLANES, d), x.dtype),
                            pltpu.SemaphoreType.DMA(()), pltpu.SemaphoreType.DMA(())],
        ),
        out_shape=jax.ShapeDtypeStruct((n_experts * cap, d), x.dtype),
        compiler_params=_cparams(("arbitrary",)),
        name="dispatch",
    )(counts, dest, x)


def _expert_kernel(blk_ref, exp_ref, valid_ref, x_ref, wg_ref, wu_ref, wd_ref, o_ref):
    s = pl.program_id(0)

    @pl.when(valid_ref[s] == 1)
    def _():
        hi, lo = _unpack_rows(x_ref[...])
        x = jnp.concatenate([hi.astype(BF16), lo.astype(BF16)], axis=1)
        gate = _dot(x, wg_ref[...])
        up = _dot(x, wu_ref[...])
        act = (jax.nn.silu(gate) * up).astype(BF16)
        o_ref[...] = _pack_rows(_dot(act, wd_ref[...]))


def expert_ffn(xs, w_gate, w_up, w_down, counts, *, cap, n_tokens):
    d = xs.shape[1]
    n_experts, f = w_gate.shape[0], w_gate.shape[2]
    tm = EXP_TILE
    n_tiles = (TOP_K * n_tokens) // tm + n_experts
    tiles_e = (counts + tm - 1) // tm
    cum = jnp.cumsum(tiles_e)
    total = cum[-1]
    slot = jnp.minimum(jnp.arange(n_tiles, dtype=jnp.int32), total - 1)
    e_of = jnp.sum((slot[:, None] >= cum[None, :]).astype(jnp.int32), axis=1)
    first = jnp.take(cum - tiles_e, e_of)
    blk = (e_of * (cap // tm) + slot - first).astype(jnp.int32)
    valid = (jnp.arange(n_tiles) < total).astype(jnp.int32)
    xmap = lambda s, blk, ex, va: (blk[s], 0)
    return pl.pallas_call(
        _expert_kernel,
        grid_spec=pltpu.PrefetchScalarGridSpec(
            num_scalar_prefetch=3,
            grid=(n_tiles,),
            in_specs=[pl.BlockSpec((tm, d), xmap),
                      pl.BlockSpec((None, 2 * d, f), lambda s, blk, ex, va: (ex[s], 0, 0)),
                      pl.BlockSpec((None, 2 * d, f), lambda s, blk, ex, va: (ex[s], 0, 0)),
                      pl.BlockSpec((None, f, 2 * d), lambda s, blk, ex, va: (ex[s], 0, 0))],
            out_specs=pl.BlockSpec((tm, d), xmap),
        ),
        out_shape=jax.ShapeDtypeStruct(xs.shape, jnp.uint32),
        compiler_params=_cparams(("arbitrary",)),
        name="expert_ffn",
    )(blk, e_of.astype(jnp.int32), valid, xs, w_gate, w_up, w_down)


def _combine_kernel(dest_ref, next_dest_ref, h_ref, gate_ref, g_ref, b_ref, ys_ref, o_ref, ob_ref, buf_ref, sems,
                    *, alpha):
    i = pl.program_id(0)
    last = pl.num_programs(0) - 1
    tm = h_ref.shape[0]
    parity = lax.rem(i, 2)

    def gather(idx_ref, to_slot):
        for r in range(tm):
            for k in range(TOP_K):
                _row_copy(ys_ref, idx_ref[k, r], buf_ref.at[to_slot, k], r, sems.at[to_slot]).start(priority=k)

    def wait_slot(slot):
        def wait(r, c):
            for k in range(TOP_K):
                _row_copy(ys_ref, 0, buf_ref.at[slot, k], 0, sems.at[slot]).wait()
            return c

        lax.fori_loop(0, tm, wait, 0, unroll=WAIT_UNROLL)

    @pl.when(i == 0)
    def _():
        gather(dest_ref, 0)

    def step(cur):
        wait_slot(cur)
        gather(next_dest_ref, 1 - cur)
        gates = gate_ref[...]
        hi0, lo0 = _unpack_rows(buf_ref[cur, 0])
        hi1, lo1 = _unpack_rows(buf_ref[cur, 1])
        ffn = jnp.concatenate([gates[:, 0:1] * hi0 + gates[:, 1:2] * hi1,
                               gates[:, 0:1] * lo0 + gates[:, 1:2] * lo1], axis=1)
        x = alpha * h_ref[...] + ffn
        y = _layer_norm_rows(x, g_ref[...], b_ref[...])
        o_ref[...] = y
        ob_ref[...] = y.astype(BF16)

        @pl.when(i == last)
        def _():
            wait_slot(1 - cur)

    for cur in range(2):
        pl.when(parity == cur)(functools.partial(step, cur))


def combine(h, dest, gates_cols, ln_g, ln_b, ys, *, alpha):
    n, d = h.shape
    nb, _, tm = dest.shape
    row = lambda i: (i, 0)
    const = lambda i: (0, 0)
    return pl.pallas_call(
        functools.partial(_combine_kernel, alpha=alpha),
        grid=(nb,),
        in_specs=[pl.BlockSpec((None, TOP_K, tm), lambda i: (i, 0, 0), memory_space=pltpu.SMEM),
                  pl.BlockSpec((None, TOP_K, tm), lambda i: (jnp.minimum(i + 1, nb - 1), 0, 0),
                               memory_space=pltpu.SMEM),
                  pl.BlockSpec((tm, d), row), pl.BlockSpec((tm, TOP_K), row),
                  pl.BlockSpec((1, d), const), pl.BlockSpec((1, d), const),
                  pl.BlockSpec(memory_space=pl.ANY)],
        out_specs=[pl.BlockSpec((tm, d), row), pl.BlockSpec((tm, d), row)],
        out_shape=[jax.ShapeDtypeStruct((n, d), F32), jax.ShapeDtypeStruct((n, d), BF16)],
        scratch_shapes=[pltpu.VMEM((2, TOP_K, tm, d // 2), jnp.uint32), pltpu.SemaphoreType.DMA((2,))],
        compiler_params=_cparams(("arbitrary",)),
        name="combine",
    )(dest, dest, h, gates_cols, ln_g.reshape(1, d), ln_b.reshape(1, d), ys)


def moe_block(h, mix, ln_g, ln_b, router_wt, router_bias, w_gate, w_up, w_down, *, alpha):
    n, d = h.shape
    cap = n + EXP_TILE
    h1, h1p, dest, gates, cnt = ln_router(h, mix, ln_g[0], ln_b[0], router_wt, router_bias,
                                          alpha=alpha, cap=cap)
    counts = cnt[:, 0]
    xs = dispatch(h1p, dest, counts, cap=cap)
    ys = expert_ffn(xs, w_gate, w_up, w_down, counts, cap=cap, n_tokens=n)
    gates_cols = jnp.transpose(gates, (0, 2, 1)).reshape(n, TOP_K)
    return combine(h1, dest, gates_cols, ln_g[1], ln_b[1], ys, alpha=alpha)


def _chunk_masks(n, chunk):
    ri = lax.broadcasted_iota(jnp.int32, (n, n), 0)
    ci = lax.broadcasted_iota(jnp.int32, (n, n), 1)
    same = (ri // chunk) == (ci // chunk)
    return same & (ri >= ci), same & (ri > ci)


def _gate_params_kernel(h_ref, w_ref, alog_ref, dtb_ref, o_ref, *, n_heads):
    tm = h_ref.shape[0]
    h_hi, h_lo = _split_bf16(h_ref[...])
    w_hi, w_lo = _split_bf16(w_ref[...])
    ab = _dot(h_hi, w_hi) + (_dot(h_hi, w_lo) + _dot(h_lo, w_hi))
    g = -jnp.exp(alog_ref[...]) * jax.nn.softplus(ab + dtb_ref[...])
    beta = jax.nn.sigmoid(ab)
    causal, _ = _chunk_masks(tm, DN_CHUNK)
    tri = jnp.where(causal, 1.0, 0.0).astype(BF16)
    g_hi, g_rest = _split_bf16(g)
    g_mid, g_lo = _split_bf16(g - g_hi.astype(F32))
    del g_rest
    gc = _dot(tri, g_hi) + (_dot(tri, g_mid) + _dot(tri, g_lo))
    lane = lax.broadcasted_iota(jnp.int32, ab.shape, 1)
    o_ref[...] = jnp.where(lane < n_heads, gc, beta)


def gate_params(h, w_ab, a_log, dt_bias):
    n, d = h.shape
    n_heads = a_log.shape[0]
    tm = min(AB_TILE, n)
    pad = V7X_LANES - 2 * n_heads
    w = jnp.pad(w_ab, ((0, 0), (0, pad)))
    lanes = lambda v: jnp.pad(v.astype(F32), (0, V7X_LANES - n_heads)).reshape(1, V7X_LANES)
    const = lambda i: (0, 0)
    return pl.pallas_call(
        functools.partial(_gate_params_kernel, n_heads=n_heads),
        grid=(n // tm,),
        in_specs=[pl.BlockSpec((tm, d), lambda i: (i, 0)), pl.BlockSpec((d, V7X_LANES), const),
                  pl.BlockSpec((1, V7X_LANES), const), pl.BlockSpec((1, V7X_LANES), const)],
        out_specs=pl.BlockSpec((tm, V7X_LANES), lambda i: (i, 0)),
        out_shape=jax.ShapeDtypeStruct((n, V7X_LANES), F32),
        compiler_params=_cparams(("parallel",)),
        name="gate_params",
    )(h, w, lanes(a_log), lanes(dt_bias))


def _unit_lower_inverses(mats):
    n = mats[0].shape[0]
    ri = lax.broadcasted_iota(jnp.int32, (n, n), 0)
    ci = lax.broadcasted_iota(jnp.int32, (n, n), 1)
    differ = ri ^ ci
    eye = jnp.where(ri == ci, 1.0, 0.0)
    base = (differ >> 1) == 0
    invs = [eye - jnp.where(base, a, 0.0) for a in mats]
    level = 1
    while (1 << level) < DN_CHUNK:
        at_level = (differ >> level) == 1
        invbs = [inv.astype(BF16) for inv in invs]
        halves = [_dot(jnp.where(at_level, a, 0.0).astype(BF16), invb).astype(BF16)
                  for a, invb in zip(mats, invbs)]
        invs = [inv - _dot(invb, half) for inv, invb, half in zip(invs, invbs, halves)]
        level += 1
    return invs


def _deltanet_kernel(q_ref, k_ref, v_ref, z_ref, cq_ref, ck_ref, cv_ref, gb_ref, gt_ref, ng_ref, y_any,
                     o_ref, s_ref, carry_ref, *, hb, dk):
    del y_any
    t = pl.program_id(2)
    tb = q_ref.shape[0]
    scale = dk ** -0.5

    @pl.when(t == 0)
    def _():
        s_ref[...] = jnp.zeros_like(s_ref)
        carry_ref[...] = jnp.zeros_like(carry_ref)

    def conv_silu(x_ref, w_ref, slot):
        raw = x_ref[...].astype(F32)
        ext = jnp.concatenate([carry_ref[slot], raw], axis=0)
        w = w_ref[...]
        taps = w.shape[0]
        acc = None
        for j in range(taps):
            lag = taps - 1 - j
            window = pltpu.roll(ext, lag, axis=0)[V7X_SUBLANES:] if lag else raw
            term = window * w[j:j + 1, :]
            acc = term if acc is None else acc + term
        carry_ref[slot] = raw[tb - V7X_SUBLANES:tb]
        return jax.nn.silu(acc)

    qa = conv_silu(q_ref, cq_ref, 0)
    ka = conv_silu(k_ref, ck_ref, 1)
    va = conv_silu(v_ref, cv_ref, 2)
    za = z_ref[...].astype(F32)
    gbl = gb_ref[...]
    gt = gt_ref[...]
    sr = min(DN_SOLVE, tb)
    n_sub = tb // sr
    causal, strict = _chunk_masks(sr, DN_CHUNK)

    heads = range(hb)
    n_chunks = tb // DN_CHUNK
    qn, kn, gc_col, a, rhs, qk = [], [], [], [], [], []
    for hh in heads:
        sl = slice(hh * dk, (hh + 1) * dk)
        q, k, v = qa[:, sl], ka[:, sl], va[:, sl]
        qn.append(q * lax.rsqrt(jnp.sum(q * q, axis=-1, keepdims=True) + RMS_EPS) * scale)
        kn.append(k * lax.rsqrt(jnp.sum(k * k, axis=-1, keepdims=True) + RMS_EPS))
        gc_col.append(gbl[:, hh:hh + 1])
        beta = gbl[:, hb + hh:hb + hh + 1]
        kb = kn[hh] * beta
        rhs.append(jnp.concatenate([v * beta, kb * jnp.exp(gc_col[hh])], axis=1).astype(BF16))
        for p in range(n_sub):
            rows = slice(p * sr, (p + 1) * sr)
            gc_row = gt[hh:hh + 1, rows]
            decay = jnp.where(causal, jnp.exp(jnp.where(causal, gc_col[hh][rows] - gc_row, 0.0)), 0.0)
            knb = kn[hh][rows].astype(BF16)
            a.append(jnp.where(strict, _nt(kb[rows].astype(BF16), knb) * decay, 0.0))
            qk.append(jnp.where(causal, _nt(qn[hh][rows].astype(BF16), knb) * decay, 0.0).astype(BF16))
    tinv = _unit_lower_inverses(a)
    solb = [jnp.concatenate([_dot(tinv[hh * n_sub + p].astype(BF16), rhs[hh][p * sr:(p + 1) * sr]).astype(BF16)
                             for p in range(n_sub)], axis=0) for hh in heads]
    qk_sol = [jnp.concatenate([_dot(qk[hh * n_sub + p], solb[hh][p * sr:(p + 1) * sr]) for p in range(n_sub)],
                              axis=0) for hh in heads]
    q_eff = [(qn[hh] * jnp.exp(gc_col[hh]) - qk_sol[hh][:, dk:]).astype(BF16) for hh in heads]
    g_last = [[gc_col[hh][(c + 1) * DN_CHUNK - 1:(c + 1) * DN_CHUNK, :] for c in range(n_chunks)] for hh in heads]
    kd_sol = [[_tn((kn[hh][c * DN_CHUNK:(c + 1) * DN_CHUNK]
                    * jnp.exp(g_last[hh][c] - gc_col[hh][c * DN_CHUNK:(c + 1) * DN_CHUNK])).astype(BF16),
                   solb[hh][c * DN_CHUNK:(c + 1) * DN_CHUNK])
               for c in range(n_chunks)] for hh in heads]
    s = [s_ref[hh] for hh in heads]
    outs = [[] for _ in heads]
    for c in range(n_chunks):
        lo, hi = c * DN_CHUNK, (c + 1) * DN_CHUNK
        for hh in heads:
            sb = s[hh].astype(BF16)
            outs[hh].append(qk_sol[hh][lo:hi, :dk] + _dot(q_eff[hh][lo:hi], sb))
            s[hh] = (s[hh] * jnp.exp(g_last[hh][c]) - _dot(kd_sol[hh][c][:, dk:].astype(BF16), sb)
                     + kd_sol[hh][c][:, :dk])
    for hh in heads:
        s_ref[hh] = s[hh]
        o = jnp.concatenate(outs[hh], axis=0)
        o = o * lax.rsqrt(jnp.mean(o * o, axis=-1, keepdims=True) + RMS_EPS) * ng_ref[...]
        o = o * jax.nn.silu(za[:, hh * dk:(hh + 1) * dk])
        o_ref[:, hh * dk:(hh + 1) * dk] = o.astype(o_ref.dtype)


def deltanet(qkvz, conv_w, gparams, norm_g, y, *, batch, n_heads):
    n = qkvz.shape[0]
    dk = norm_g.shape[0]
    hb = min(DN_HB, n_heads)
    w = hb * dk
    nhb = n_heads // hb
    seq = n // batch
    tb = min(DN_TB, seq)
    nt = seq // tb
    gc = gparams[:, :n_heads].reshape(n, nhb, hb)
    beta = gparams[:, n_heads:2 * n_heads].reshape(n, nhb, hb)
    cols = jnp.concatenate([gc, beta], axis=-1).transpose(1, 0, 2)
    cols = jnp.pad(cols, ((0, 0), (0, 0), (0, V7X_LANES - 2 * hb)))
    rows = jnp.pad(gc.transpose(1, 2, 0), ((0, 0), (0, V7X_SUBLANES - hb), (0, 0)))
    rmap = lambda part: (lambda b, h, t: (b * nt + t, part * nhb + h))
    cmap = lambda part: (lambda b, h, t: (0, part * nhb + h))
    taps = conv_w.shape[0]
    return pl.pallas_call(
        functools.partial(_deltanet_kernel, hb=hb, dk=dk),
        grid=(batch, nhb, nt),
        in_specs=[pl.BlockSpec((tb, w), rmap(0)), pl.BlockSpec((tb, w), rmap(1)),
                  pl.BlockSpec((tb, w), rmap(2)), pl.BlockSpec((tb, w), rmap(3)),
                  pl.BlockSpec((taps, w), cmap(0)), pl.BlockSpec((taps, w), cmap(1)),
                  pl.BlockSpec((taps, w), cmap(2)),
                  pl.BlockSpec((None, tb, V7X_LANES), lambda b, h, t: (h, b * nt + t, 0)),
                  pl.BlockSpec((None, V7X_SUBLANES, tb), lambda b, h, t: (h, 0, b * nt + t)),
                  pl.BlockSpec((1, dk), lambda b, h, t: (0, 0)),
                  pl.BlockSpec(memory_space=pl.ANY)],
        out_specs=pl.BlockSpec((tb, w), rmap(0)),
        out_shape=jax.ShapeDtypeStruct(y.shape, y.dtype),
        scratch_shapes=[pltpu.VMEM((hb, dk, dk), F32), pltpu.VMEM((3, V7X_SUBLANES, w), F32)],
        input_output_aliases={10: 0},
        compiler_params=_cparams(("parallel", "parallel", "arbitrary")),
        name="deltanet",
    )(qkvz, qkvz, qkvz, qkvz, conv_w, conv_w, conv_w, cols, rows, norm_g.reshape(1, dk), y)


POOL_HALO = 16


def _pool_kernel(x_ref, w_ref, b_ref, sc_ref, o_ref, carry_ref):
    t = pl.program_id(1)
    tt = x_ref.shape[0]
    cg = w_ref.shape[1]

    @pl.when(t == 0)
    def _():
        carry_ref[...] = jnp.zeros_like(carry_ref)

    x = x_ref[...].astype(F32)
    ext = jnp.concatenate([carry_ref[...], x], axis=0)
    carry_ref[...] = x[tt - POOL_HALO:tt]
    pos = t * tt + lax.broadcasted_iota(jnp.int32, (tt, 1), 0)
    for gi, win in enumerate(POOL_WINDOWS):
        cols = slice(gi * cg, (gi + 1) * cg)
        cur = ext[:, cols]
        span = 1
        while span < win:
            cur = cur[span:] + cur[:cur.shape[0] - span]
            span *= 2
        lo = POOL_HALO - (win - 1)
        total = cur[lo:lo + tt]
        count = jnp.minimum(pos + 1, win).astype(F32)
        pooled = total / count - x[:, cols]
        mixed = _dot(pooled.astype(BF16), w_ref[gi]) + b_ref[:, cols]
        o_ref[:, cols] = (mixed * sc_ref[:, cols]).astype(o_ref.dtype)


def pool(xp, pool_w, pool_b, pool_scale, *, batch, out_width):
    n, p = xp.shape
    assert out_width % p == 0 and max(POOL_WINDOWS) <= POOL_HALO
    seq = n // batch
    tt = min(POOL_TILE, seq)
    nt = seq // tt
    g, cg, _ = pool_w.shape
    const = lambda b, t: (0, 0)
    return pl.pallas_call(
        _pool_kernel,
        grid=(batch, nt),
        in_specs=[pl.BlockSpec((tt, p), lambda b, t: (b * nt + t, 0)),
                  pl.BlockSpec((g, cg, cg), lambda b, t: (0, 0, 0)),
                  pl.BlockSpec((1, p), const), pl.BlockSpec((1, p), const)],
        out_specs=pl.BlockSpec((tt, p), lambda b, t: (b * nt + t, out_width // p - 1)),
        out_shape=jax.ShapeDtypeStruct((n, out_width), BF16),
        scratch_shapes=[pltpu.VMEM((POOL_HALO, p), F32)],
        compiler_params=_cparams(("parallel", "arbitrary")),
        name="pool",
    )(xp, pool_w, pool_b.reshape(1, p).astype(F32), pool_scale.reshape(1, p).astype(F32))


def _sgu_kernel(u_ref, v_ref, g_ref, b_ref, ws_ref, bs_ref, o_ref):
    tt, width = u_ref.shape
    n_heads, chunk, _ = ws_ref.shape
    hd = width // n_heads
    vn = _layer_norm_rows(v_ref[...].astype(F32), g_ref[...], b_ref[...]).astype(BF16)
    ri = lax.broadcasted_iota(jnp.int32, (chunk, chunk), 0)
    ci = lax.broadcasted_iota(jnp.int32, (chunk, chunk), 1)
    for g in range(n_heads):
        cols = slice(g * hd, (g + 1) * hd)
        wsg = jnp.where(ri >= ci, ws_ref[g], 0.0).astype(BF16)
        bias = bs_ref[:, g:g + 1]
        for c in range(tt // chunk):
            rows = slice(c * chunk, (c + 1) * chunk)
            mixed = _dot(wsg, vn[rows, cols]) + bias
            o_ref[rows, cols] = (u_ref[rows, cols].astype(F32) * mixed).astype(o_ref.dtype)


def sgu(uv, ln_g, ln_b, w_s, b_s):
    n, two_w = uv.shape
    width = two_w // 2
    n_heads, chunk, _ = w_s.shape
    tt = min(SGU_TILE, n)
    assert tt % chunk == 0
    const = lambda i: (0, 0)
    return pl.pallas_call(
        _sgu_kernel,
        grid=(n // tt,),
        in_specs=[pl.BlockSpec((tt, width), lambda i: (i, 0)), pl.BlockSpec((tt, width), lambda i: (i, 1)),
                  pl.BlockSpec((1, width), const), pl.BlockSpec((1, width), const),
                  pl.BlockSpec((n_heads, chunk, chunk), lambda i: (0, 0, 0)),
                  pl.BlockSpec((chunk, n_heads), const)],
        out_specs=pl.BlockSpec((tt, width), lambda i: (i, 0)),
        out_shape=jax.ShapeDtypeStruct((n, width), BF16),
        compiler_params=_cparams(("parallel",)),
        name="sgu",
    )(uv, uv, ln_g.reshape(1, width), ln_b.reshape(1, width), w_s, jnp.transpose(b_s))


def even_mixer(hf, hb16, w_in_b, w_ab, w_conv, a_log, dt_bias, norm_g, pool_w, pool_b, pool_scale, w_out_b,
               riders_in, riders_out, *, batch):
    n_heads = a_log.shape[0]
    dk = norm_g.shape[0]
    dn_width = n_heads * dk
    o0 = 4 * dn_width
    o1 = o0 + 2 * n_heads
    qkvz, rounded_in = matmul(hb16, w_in_b, riders=riders_in, n_cols=o0)
    xp, _ = matmul(hb16, w_in_b[:, o1:])
    gparams = gate_params(hf, w_ab, a_log, dt_bias)
    y = pool(xp, pool_w.astype(BF16), pool_b, pool_scale, batch=batch, out_width=w_out_b.shape[0])
    y = deltanet(qkvz, w_conv, gparams, norm_g, y, batch=batch, n_heads=n_heads)
    mix, rounded_out = matmul(y, w_out_b, riders=riders_out)
    return mix, rounded_in, rounded_out


def odd_mixer(hb16, w_in_b, b_in, ln_g, ln_b, w_s, b_s, w_out_b, riders_in, riders_out):
    uv, rounded_in = matmul(hb16, w_in_b, bias=b_in, act="gelu", riders=riders_in)
    gated = sgu(uv, ln_g, ln_b, w_s, b_s)
    mix, rounded_out = matmul(gated, w_out_b, riders=riders_out)
    return mix, rounded_in, rounded_out


def kernel(x, ev_w_in, ev_conv, ev_a_log, ev_dt_bias, ev_norm_g, ev_pool_w, ev_pool_b, ev_pool_scale, ev_w_out, od_w_in, od_b_in, od_ln_g, od_ln_b, od_w_s, od_b_s, od_w_out, ln_g, ln_b, router_w, router_bias, moe_w_gate, moe_w_up, moe_w_down):
    batch, seq, d = x.shape
    depth = ln_g.shape[0]
    alpha = (2.0 * depth) ** 0.25
    hf = x.reshape(batch * seq, d)
    hb16 = hf.astype(BF16)
    router_wt = jnp.transpose(router_w)
    _, n_experts, _, f = moe_w_gate.shape
    wg_rows = moe_w_gate.reshape(depth * n_experts * d, f)
    wu_rows = moe_w_up.reshape(depth * n_experts * d, f)
    wd_rows = moe_w_down.reshape(depth * n_experts * f, d)
    stack = lambda w: (w.reshape(-1, w.shape[-1]), w.shape[1])
    mix_rows = {0: (stack(ev_w_in), stack(ev_w_out)), 1: (stack(od_w_in), stack(od_w_out))}
    w_in_b, w_out_b = ev_w_in[0].astype(BF16), ev_w_out[0].astype(BF16)
    o0 = 4 * ev_a_log.shape[1] * ev_norm_g.shape[1]
    o1 = o0 + 2 * ev_a_log.shape[1]
    for layer in range(depth):
        i = layer // 2
        riders = ((wg_rows, n_experts * d, layer), (wu_rows, n_experts * d, layer),
                  (wd_rows, n_experts * f, layer))
        nxt = layer + 1
        next_riders = tuple((arr, rows, nxt // 2) for arr, rows in mix_rows[nxt % 2]) if nxt < depth else ()
        if layer % 2 == 0:
            mix, (wg, wu, wd), rounded = even_mixer(
                hf, hb16, w_in_b, ev_w_in[i, :, o0:o1], ev_conv[i], ev_a_log[i], ev_dt_bias[i], ev_norm_g[i],
                ev_pool_w[i], ev_pool_b[i], ev_pool_scale[i], w_out_b, riders, next_riders, batch=batch)
        else:
            mix, (wg, wu, wd), rounded = odd_mixer(hb16, w_in_b, od_b_in[i], od_ln_g[i], od_ln_b[i], od_w_s[i],
                                                   od_b_s[i], w_out_b, riders, next_riders)
        if rounded:
            w_in_b, w_out_b = rounded
        hf, hb16 = moe_block(hf, mix, ln_g[layer], ln_b[layer], router_wt, router_bias,
                             wg.reshape(n_experts, d, f), wu.reshape(n_experts, d, f),
                             wd.reshape(n_experts, f, d), alpha=alpha)
    return hf.reshape(batch, seq, d)
```

```python
import functools

import jax
import jax.numpy as jnp
from jax import lax
from jax.experimental import pallas as pl
from jax.experimental.pallas import tpu as pltpu

F32 = jnp.float32
BF16 = jnp.bfloat16

V7X_LANES = 128
V7X_SUBLANES = 8
V7X_VMEM_BYTES = 64 * 1024 * 1024
VMEM_LIMIT = V7X_VMEM_BYTES - 8 * 1024 * 1024

DN_CHUNK = 64
POOL_WINDOWS = (2, 4, 8, 16)
N_GROUPS = 4
TOP_K = 2
LN_EPS = 1e-5
RMS_EPS = 1e-6

MM_TM = 1024
MM_TN = 1024
TOK_TILE = 256
EXP_TILE = 512
DN_TB = 256
DN_SOLVE = 128
DN_HB = 8
ROUND_ROWS = 128
WAIT_UNROLL = 16
AB_TILE = 256
POOL_TILE = 256
SGU_TILE = 256


def _cparams(sem):
    return pltpu.CompilerParams(dimension_semantics=sem, vmem_limit_bytes=VMEM_LIMIT)


def _nt(a, b):
    return lax.dot_general(a, b, (((1,), (1,)), ((), ())), preferred_element_type=F32)


def _tn(a, b):
    return lax.dot_general(a, b, (((0,), (0,)), ((), ())), preferred_element_type=F32)


def _dot(a, b):
    return jnp.dot(a, b, preferred_element_type=F32)


def _split_bf16(x):
    hi = x.astype(BF16)
    return hi, (x - hi.astype(F32)).astype(BF16)


def _nt_split(a, b):
    a_hi, a_lo = _split_bf16(a)
    b_hi, b_lo = _split_bf16(b)
    return _nt(a_hi, b_hi) + (_nt(a_hi, b_lo) + _nt(a_lo, b_hi))


def _round_riders(rider_in, rider_out, rider_cols):
    outs = iter(rider_out)
    for src, ranges in zip(rider_in, rider_cols):
        for lo, hi in ranges:
            dst = next(outs)
            dst[...] = src[:, lo:hi].astype(dst.dtype)


def _rider_specs(riders, n_steps, step_index):
    in_specs, args, out_specs, out_shape, rider_cols = [], [], [], [], []
    for arr, rows, layer, ranges in riders:
        cols = arr.shape[1]
        ranges = ((0, cols),) if ranges is None else ranges
        rb = rows // n_steps
        assert rb * n_steps == rows and rb % (2 * V7X_SUBLANES) == 0
        in_specs.append(pl.BlockSpec((rb, cols), lambda *g, layer=layer: (layer * n_steps + step_index(*g), 0)))
        args.append(arr)
        for lo, hi in ranges:
            out_specs.append(pl.BlockSpec((rb, hi - lo), lambda *g: (step_index(*g), 0)))
            out_shape.append(jax.ShapeDtypeStruct((rows, hi - lo), BF16))
        rider_cols.append(ranges)
    return in_specs, args, out_specs, out_shape, tuple(rider_cols)


def _mm_kernel(*refs, act, has_bias, rider_cols):
    n_in = 3 if has_bias else 2
    n_riders = len(rider_cols)
    x_ref, w_ref = refs[0], refs[1]
    o_ref = refs[n_in + n_riders]
    acc = _dot(x_ref[...], w_ref[...])
    if has_bias:
        acc = acc + refs[2][...]
    if act == "gelu":
        acc = jax.nn.gelu(acc)
    o_ref[...] = acc.astype(o_ref.dtype)
    _round_riders(refs[n_in:n_in + n_riders], refs[n_in + n_riders + 1:], rider_cols)


def matmul(x, w, bias=None, act=None, out_dtype=BF16, riders=()):
    m, k = x.shape
    n = w.shape[1]
    tm, tn = min(MM_TM, m), min(MM_TN, n)
    assert m % tm == 0 and n % tn == 0
    ni, nj = m // tm, n // tn
    in_specs = [pl.BlockSpec((tm, k), lambda i, j: (i, 0)),
                pl.BlockSpec((k, tn), lambda i, j: (0, j))]
    args = [x, w]
    if bias is not None:
        in_specs.append(pl.BlockSpec((1, tn), lambda i, j: (0, j)))
        args.append(bias.reshape(1, n).astype(F32))
    r_in, r_args, r_out, r_shape, rider_cols = _rider_specs(riders, ni * nj, lambda i, j: i * nj + j)
    outs = pl.pallas_call(
        functools.partial(_mm_kernel, act=act, has_bias=bias is not None, rider_cols=rider_cols),
        grid=(ni, nj),
        in_specs=in_specs + r_in,
        out_specs=[pl.BlockSpec((tm, tn), lambda i, j: (i, j))] + r_out,
        out_shape=[jax.ShapeDtypeStruct((m, n), out_dtype)] + r_shape,
        compiler_params=_cparams(("parallel", "arbitrary")),
        name="matmul",
    )(*args, *r_args)
    return outs[0], list(outs[1:])


def _round_kernel(*refs, rider_cols):
    n = len(rider_cols)
    _round_riders(refs[:n], refs[n:], rider_cols)


def round_weights(riders):
    n_steps = min(rows // ROUND_ROWS for _, rows, _, _ in riders)
    r_in, r_args, r_out, r_shape, rider_cols = _rider_specs(riders, n_steps, lambda s: s)
    return pl.pallas_call(
        functools.partial(_round_kernel, rider_cols=rider_cols),
        grid=(n_steps,),
        in_specs=r_in,
        out_specs=r_out,
        out_shape=r_shape,
        compiler_params=_cparams(("parallel",)),
        name="round_weights",
    )(*r_args)


def _pack_rows(x):
    half = x.shape[1] // 2
    bits = lambda v: lax.bitcast_convert_type(v.astype(BF16).astype(F32), jnp.uint32)
    return bits(x[:, :half]) | (bits(x[:, half:]) >> 16)


def _unpack_rows(words):
    hi = lax.bitcast_convert_type(words & jnp.uint32(0xFFFF0000), F32)
    lo = lax.bitcast_convert_type(words << 16, F32)
    return hi, lo


def _layer_norm_rows(x, g, b):
    mu = jnp.mean(x, axis=-1, keepdims=True)
    xc = x - mu
    var = jnp.mean(xc * xc, axis=-1, keepdims=True)
    return xc * lax.rsqrt(var + LN_EPS) * g + b


def _ln_router_kernel(h_ref, mix_ref, g_ref, b_ref, rwt_ref, rb_ref,
                      h1_ref, h1p_ref, dest_ref, gate_ref, cnt_ref, carry_ref,
                      *, alpha, n_experts, cap):
    i = pl.program_id(0)
    epg = n_experts // N_GROUPS
    tm = h_ref.shape[0]

    @pl.when(i == 0)
    def _():
        carry_ref[...] = jnp.zeros_like(carry_ref)

    x = alpha * h_ref[...] + mix_ref[...].astype(F32)
    y = _layer_norm_rows(x, g_ref[...], b_ref[...])
    h1_ref[...] = y
    h1p_ref[...] = _pack_rows(y)

    logits = _nt_split(rwt_ref[...], y)
    mx = jnp.max(logits, axis=0, keepdims=True)
    ex = jnp.exp(logits - mx)
    probs = ex / jnp.sum(ex, axis=0, keepdims=True)
    sel = probs + rb_ref[...]

    srow = [sel[e:e + 1, :] for e in range(n_experts)]
    prow = [probs[e:e + 1, :] for e in range(n_experts)]

    def top2_sum(vals):
        best = None
        for a in range(len(vals)):
            for c in range(a + 1, len(vals)):
                s = vals[a] + vals[c]
                best = s if best is None else jnp.maximum(best, s)
        return best

    gscore = [top2_sum(srow[g * epg:(g + 1) * epg]) for g in range(N_GROUPS)]
    best_g = jnp.zeros_like(gscore[0], dtype=jnp.int32)
    best_s = gscore[0]
    for g in range(1, N_GROUPS):
        upd = gscore[g] > best_s
        best_g = jnp.where(upd, g, best_g)
        best_s = jnp.where(upd, gscore[g], best_s)

    def pick(rows, j):
        out = rows[j]
        for g in range(1, N_GROUPS):
            out = jnp.where(best_g == g, rows[g * epg + j], out)
        return out

    in_s = [pick(srow, j) for j in range(epg)]
    in_p = [pick(prow, j) for j in range(epg)]

    def argmax_first(vals, exclude=None):
        bi = None
        bv = None
        for j, v in enumerate(vals):
            if exclude is not None:
                v = jnp.where(exclude == j, -jnp.inf, v)
            if bi is None:
                bi = jnp.zeros_like(best_g)
                bv = v
            else:
                upd = v > bv
                bi = jnp.where(upd, j, bi)
                bv = jnp.where(upd, v, bv)
        return bi

    loc0 = argmax_first(in_s)
    loc1 = argmax_first(in_s, exclude=loc0)

    def take(vals, idx):
        out = vals[0]
        for j in range(1, len(vals)):
            out = jnp.where(idx == j, vals[j], out)
        return out

    p0 = take(in_p, loc0)
    p1 = take(in_p, loc1)
    psum = p0 + p1
    gate_ref[...] = jnp.concatenate([p0 / psum, p1 / psum], axis=0)[None]

    idx0 = best_g * epg + loc0
    idx1 = best_g * epg + loc1

    eid = lax.broadcasted_iota(jnp.int32, (n_experts, tm), 0)
    hit0 = eid == idx0
    hit1 = eid == idx1
    member = jnp.where(hit0, 1.0, jnp.where(hit1, 1.0, 0.0))
    srci = lax.broadcasted_iota(jnp.int32, (tm, tm), 0)
    dsti = lax.broadcasted_iota(jnp.int32, (tm, tm), 1)
    upper = jnp.where(srci < dsti, 1.0, 0.0).astype(BF16)
    prefix = _dot(member.astype(BF16), upper)
    carry = carry_ref[...]
    pos = eid.astype(F32) * float(cap) + carry[:, :1] + prefix
    dest0 = jnp.sum(jnp.where(hit0, pos, 0.0), axis=0, keepdims=True)
    dest1 = jnp.sum(jnp.where(hit1, pos, 0.0), axis=0, keepdims=True)
    dest_ref[...] = jnp.concatenate([dest0, dest1], axis=0).astype(jnp.int32)[None]
    carry = carry + jnp.sum(member, axis=1, keepdims=True)
    carry_ref[...] = carry
    cnt_ref[...] = carry.astype(jnp.int32)


def ln_router(h, mix, ln_g, ln_b, router_wt, router_bias, *, alpha, cap):
    n, d = h.shape
    n_experts = router_wt.shape[0]
    tm = min(TOK_TILE, n)
    nb = n // tm
    row = lambda i: (i, 0)
    const = lambda i: (0, 0)
    return pl.pallas_call(
        functools.partial(_ln_router_kernel, alpha=alpha, n_experts=n_experts, cap=cap),
        grid=(nb,),
        in_specs=[pl.BlockSpec((tm, d), row), pl.BlockSpec((tm, d), row),
                  pl.BlockSpec((1, d), const), pl.BlockSpec((1, d), const),
                  pl.BlockSpec((n_experts, d), const), pl.BlockSpec((n_experts, 1), const)],
        out_specs=[pl.BlockSpec((tm, d), row), pl.BlockSpec((tm, d // 2), row),
                   pl.BlockSpec((1, TOP_K, tm), lambda i: (i, 0, 0)),
                   pl.BlockSpec((1, TOP_K, tm), lambda i: (i, 0, 0)),
                   pl.BlockSpec((n_experts, V7X_LANES), const)],
        out_shape=[jax.ShapeDtypeStruct((n, d), F32), jax.ShapeDtypeStruct((n, d // 2), jnp.uint32),
                   jax.ShapeDtypeStruct((nb, TOP_K, tm), jnp.int32),
                   jax.ShapeDtypeStruct((nb, TOP_K, tm), F32),
                   jax.ShapeDtypeStruct((n_experts, V7X_LANES), jnp.int32)],
        scratch_shapes=[pltpu.VMEM((n_experts, V7X_LANES), F32)],
        compiler_params=_cparams(("arbitrary",)),
        name="ln_router",
    )(h, mix, ln_g.reshape(1, d), ln_b.reshape(1, d), router_wt, router_bias.reshape(n_experts, 1))


def _row_copy(src_ref, src_row, dst_ref, dst_row, sem):
    return pltpu.make_async_copy(src_ref.at[pl.ds(src_row, 1), :], dst_ref.at[pl.ds(dst_row, 1), :], sem)


def _dispatch_kernel(cnt_ref, dest_ref, x_ref, xs_ref, zero_ref, sem, zsem, *, cap, n_experts):
    i = pl.program_id(0)
    tm = x_ref.shape[0]

    for r in range(tm):
        for k in range(TOP_K):
            _row_copy(x_ref, r, xs_ref, dest_ref[k, r], sem).start(priority=k)

    @pl.when(i == pl.num_programs(0) - 1)
    def _():
        zero_ref[...] = jnp.zeros_like(zero_ref)

        def pad_rows(e, do):
            cnt = cnt_ref[e]
            n_pad = lax.rem(EXP_TILE - lax.rem(cnt, EXP_TILE), EXP_TILE)

            def one(r, c):
                do(_row_copy(zero_ref, 0, xs_ref, e * cap + cnt + r, zsem))
                return c

            lax.fori_loop(0, n_pad, one, 0)
            return do

        lax.fori_loop(0, n_experts, lambda e, c: (pad_rows(e, lambda cp: cp.start()), c)[1], 0)
        lax.fori_loop(0, n_experts, lambda e, c: (pad_rows(e, lambda cp: cp.wait()), c)[1], 0)

    def wait(r, c):
        for k in range(TOP_K):
            _row_copy(x_ref, 0, xs_ref, 0, sem).wait()
        return c

    lax.fori_loop(0, tm, wait, 0, unroll=WAIT_UNROLL)


def dispatch(x, dest, counts, *, cap):
    n, d = x.shape
    n_experts = counts.shape[0]
    nb, _, tm = dest.shape
    return pl.pallas_call(
        functools.partial(_dispatch_kernel, cap=cap, n_experts=n_experts),
        grid_spec=pltpu.PrefetchScalarGridSpec(
            num_scalar_prefetch=1,
            grid=(nb,),
            in_specs=[pl.BlockSpec((None, TOP_K, tm), lambda i, c: (i, 0, 0), memory_space=pltpu.SMEM),
                      pl.BlockSpec((tm, d), lambda i, c: (i, 0))],
            out_specs=pl.BlockSpec(memory_space=pl.ANY),
            scratch_shapes=[pltpu.VMEM((V7X_SUBLANES, d), x.dtype),
                            pltpu.SemaphoreType.DMA(()), pltpu.SemaphoreType.DMA(())],
        ),
        out_shape=jax.ShapeDtypeStruct((n_experts * cap, d), x.dtype),
        compiler_params=_cparams(("arbitrary",)),
        name="dispatch",
    )(counts, dest, x)


def _expert_kernel(blk_ref, exp_ref, valid_ref, x_ref, wg_ref, wu_ref, wd_ref, o_ref):
    s = pl.program_id(0)

    @pl.when(valid_ref[s] == 1)
    def _():
        hi, lo = _unpack_rows(x_ref[...])
        x = jnp.concatenate([hi.astype(BF16), lo.astype(BF16)], axis=1)
        gate = _dot(x, wg_ref[...])
        up = _dot(x, wu_ref[...])
        act = (jax.nn.silu(gate) * up).astype(BF16)
        o_ref[...] = _pack_rows(_dot(act, wd_ref[...]))


def expert_ffn(xs, w_gate, w_up, w_down, counts, *, cap, n_tokens):
    d = xs.shape[1]
    n_experts, f = w_gate.shape[0], w_gate.shape[2]
    tm = EXP_TILE
    n_tiles = (TOP_K * n_tokens) // tm + n_experts
    tiles_e = (counts + tm - 1) // tm
    cum = jnp.cumsum(tiles_e)
    total = cum[-1]
    slot = jnp.minimum(jnp.arange(n_tiles, dtype=jnp.int32), total - 1)
    e_of = jnp.sum((slot[:, None] >= cum[None, :]).astype(jnp.int32), axis=1)
    first = jnp.take(cum - tiles_e, e_of)
    blk = (e_of * (cap // tm) + slot - first).astype(jnp.int32)
    valid = (jnp.arange(n_tiles) < total).astype(jnp.int32)
    xmap = lambda s, blk, ex, va: (blk[s], 0)
    return pl.pallas_call(
        _expert_kernel,
        grid_spec=pltpu.PrefetchScalarGridSpec(
            num_scalar_prefetch=3,
            grid=(n_tiles,),
            in_specs=[pl.BlockSpec((tm, d), xmap),
                      pl.BlockSpec((None, 2 * d, f), lambda s, blk, ex, va: (ex[s], 0, 0)),
                      pl.BlockSpec((None, 2 * d, f), lambda s, blk, ex, va: (ex[s], 0, 0)),
                      pl.BlockSpec((None, f, 2 * d), lambda s, blk, ex, va: (ex[s], 0, 0))],
            out_specs=pl.BlockSpec((tm, d), xmap),
        ),
        out_shape=jax.ShapeDtypeStruct(xs.shape, jnp.uint32),
        compiler_params=_cparams(("arbitrary",)),
        name="expert_ffn",
    )(blk, e_of.astype(jnp.int32), valid, xs, w_gate, w_up, w_down)


def _combine_kernel(dest_ref, next_dest_ref, h_ref, gate_ref, g_ref, b_ref, ys_ref, o_ref, ob_ref, buf_ref, sems,
                    *, alpha):
    i = pl.program_id(0)
    last = pl.num_programs(0) - 1
    tm = h_ref.shape[0]
    parity = lax.rem(i, 2)

    def gather(idx_ref, to_slot):
        for r in range(tm):
            for k in range(TOP_K):
                _row_copy(ys_ref, idx_ref[k, r], buf_ref.at[to_slot, k], r, sems.at[to_slot]).start(priority=k)

    def wait_slot(slot):
        def wait(r, c):
            for k in range(TOP_K):
                _row_copy(ys_ref, 0, buf_ref.at[slot, k], 0, sems.at[slot]).wait()
            return c

        lax.fori_loop(0, tm, wait, 0, unroll=WAIT_UNROLL)

    @pl.when(i == 0)
    def _():
        gather(dest_ref, 0)

    def step(cur):
        wait_slot(cur)
        gather(next_dest_ref, 1 - cur)
        gates = gate_ref[...]
        hi0, lo0 = _unpack_rows(buf_ref[cur, 0])
        hi1, lo1 = _unpack_rows(buf_ref[cur, 1])
        ffn = jnp.concatenate([gates[:, 0:1] * hi0 + gates[:, 1:2] * hi1,
                               gates[:, 0:1] * lo0 + gates[:, 1:2] * lo1], axis=1)
        x = alpha * h_ref[...] + ffn
        y = _layer_norm_rows(x, g_ref[...], b_ref[...])
        o_ref[...] = y
        ob_ref[...] = y.astype(BF16)

        @pl.when(i == last)
        def _():
            wait_slot(1 - cur)

    for cur in range(2):
        pl.when(parity == cur)(functools.partial(step, cur))


def combine(h, dest, gates_cols, ln_g, ln_b, ys, *, alpha):
    n, d = h.shape
    nb, _, tm = dest.shape
    row = lambda i: (i, 0)
    const = lambda i: (0, 0)
    return pl.pallas_call(
        functools.partial(_combine_kernel, alpha=alpha),
        grid=(nb,),
        in_specs=[pl.BlockSpec((None, TOP_K, tm), lambda i: (i, 0, 0), memory_space=pltpu.SMEM),
                  pl.BlockSpec((None, TOP_K, tm), lambda i: (jnp.minimum(i + 1, nb - 1), 0, 0),
                               memory_space=pltpu.SMEM),
                  pl.BlockSpec((tm, d), row), pl.BlockSpec((tm, TOP_K), row),
                  pl.BlockSpec((1, d), const), pl.BlockSpec((1, d), const),
                  pl.BlockSpec(memory_space=pl.ANY)],
        out_specs=[pl.BlockSpec((tm, d), row), pl.BlockSpec((tm, d), row)],
        out_shape=[jax.ShapeDtypeStruct((n, d), F32), jax.ShapeDtypeStruct((n, d), BF16)],
        scratch_shapes=[pltpu.VMEM((2, TOP_K, tm, d // 2), jnp.uint32), pltpu.SemaphoreType.DMA((2,))],
        compiler_params=_cparams(("arbitrary",)),
        name="combine",
    )(dest, dest, h, gates_cols, ln_g.reshape(1, d), ln_b.reshape(1, d), ys)


def moe_block(h, mix, ln_g, ln_b, router_wt, router_bias, w_gate, w_up, w_down, *, alpha):
    n, d = h.shape
    cap = n + EXP_TILE
    h1, h1p, dest, gates, cnt = ln_router(h, mix, ln_g[0], ln_b[0], router_wt, router_bias,
                                          alpha=alpha, cap=cap)
    counts = cnt[:, 0]
    xs = dispatch(h1p, dest, counts, cap=cap)
    ys = expert_ffn(xs, w_gate, w_up, w_down, counts, cap=cap, n_tokens=n)
    gates_cols = jnp.transpose(gates, (0, 2, 1)).reshape(n, TOP_K)
    return combine(h1, dest, gates_cols, ln_g[1], ln_b[1], ys, alpha=alpha)


def _chunk_masks(n, chunk):
    ri = lax.broadcasted_iota(jnp.int32, (n, n), 0)
    ci = lax.broadcasted_iota(jnp.int32, (n, n), 1)
    same = (ri // chunk) == (ci // chunk)
    return same & (ri >= ci), same & (ri > ci)


def _gate_params_kernel(h_ref, w_ref, alog_ref, dtb_ref, o_ref, *, n_heads):
    tm = h_ref.shape[0]
    h_hi, h_lo = _split_bf16(h_ref[...])
    w_hi, w_lo = _split_bf16(w_ref[...])
    ab = _dot(h_hi, w_hi) + (_dot(h_hi, w_lo) + _dot(h_lo, w_hi))
    g = -jnp.exp(alog_ref[...]) * jax.nn.softplus(ab + dtb_ref[...])
    beta = jax.nn.sigmoid(ab)
    causal, _ = _chunk_masks(tm, DN_CHUNK)
    tri = jnp.where(causal, 1.0, 0.0).astype(BF16)
    g_hi, g_rest = _split_bf16(g)
    g_mid, g_lo = _split_bf16(g - g_hi.astype(F32))
    del g_rest
    gc = _dot(tri, g_hi) + (_dot(tri, g_mid) + _dot(tri, g_lo))
    lane = lax.broadcasted_iota(jnp.int32, ab.shape, 1)
    o_ref[...] = jnp.where(lane < n_heads, gc, beta)


def gate_params(h, w_in_all, layer, col0, a_log, dt_bias):
    n, d = h.shape
    n_heads = a_log.shape[0]
    assert col0 % V7X_LANES == 0 and 2 * n_heads <= V7X_LANES
    tm = min(AB_TILE, n)
    lanes = lambda v: jnp.pad(v.astype(F32), (0, V7X_LANES - n_heads)).reshape(1, V7X_LANES)
    const = lambda i: (0, 0)
    return pl.pallas_call(
        functools.partial(_gate_params_kernel, n_heads=n_heads),
        grid=(n // tm,),
        in_specs=[pl.BlockSpec((tm, d), lambda i: (i, 0)),
                  pl.BlockSpec((None, d, V7X_LANES), lambda i: (layer, 0, col0 // V7X_LANES)),
                  pl.BlockSpec((1, V7X_LANES), const), pl.BlockSpec((1, V7X_LANES), const)],
        out_specs=pl.BlockSpec((tm, V7X_LANES), lambda i: (i, 0)),
        out_shape=jax.ShapeDtypeStruct((n, V7X_LANES), F32),
        compiler_params=_cparams(("parallel",)),
        name="gate_params",
    )(h, w_in_all, lanes(a_log), lanes(dt_bias))


def _unit_lower_inverses(mats):
    n = mats[0].shape[0]
    ri = lax.broadcasted_iota(jnp.int32, (n, n), 0)
    ci = lax.broadcasted_iota(jnp.int32, (n, n), 1)
    differ = ri ^ ci
    eye = jnp.where(ri == ci, 1.0, 0.0)
    base = (differ >> 1) == 0
    invs = [eye - jnp.where(base, a, 0.0) for a in mats]
    level = 1
    while (1 << level) < DN_CHUNK:
        at_level = (differ >> level) == 1
        invbs = [inv.astype(BF16) for inv in invs]
        halves = [_dot(jnp.where(at_level, a, 0.0).astype(BF16), invb).astype(BF16)
                  for a, invb in zip(mats, invbs)]
        invs = [inv - _dot(invb, half) for inv, invb, half in zip(invs, invbs, halves)]
        level += 1
    return invs


def _deltanet_kernel(q_ref, k_ref, v_ref, z_ref, cq_ref, ck_ref, cv_ref, gb_ref, gt_ref, ng_ref, y_any,
                     o_ref, s_ref, carry_ref, *, hb, dk):
    del y_any
    t = pl.program_id(2)
    tb = q_ref.shape[0]
    scale = dk ** -0.5

    @pl.when(t == 0)
    def _():
        s_ref[...] = jnp.zeros_like(s_ref)
        carry_ref[...] = jnp.zeros_like(carry_ref)

    def conv_silu(x_ref, w_ref, slot):
        raw = x_ref[...].astype(F32)
        ext = jnp.concatenate([carry_ref[slot], raw], axis=0)
        w = w_ref[...]
        taps = w.shape[0]
        acc = None
        for j in range(taps):
            lag = taps - 1 - j
            window = pltpu.roll(ext, lag, axis=0)[V7X_SUBLANES:] if lag else raw
            term = window * w[j:j + 1, :]
            acc = term if acc is None else acc + term
        carry_ref[slot] = raw[tb - V7X_SUBLANES:tb]
        return jax.nn.silu(acc)

    qa = conv_silu(q_ref, cq_ref, 0)
    ka = conv_silu(k_ref, ck_ref, 1)
    va = conv_silu(v_ref, cv_ref, 2)
    za = z_ref[...].astype(F32)
    gbl = gb_ref[...]
    gt = gt_ref[...]
    sr = min(DN_SOLVE, tb)
    n_sub = tb // sr
    causal, strict = _chunk_masks(sr, DN_CHUNK)

    heads = range(hb)
    n_chunks = tb // DN_CHUNK
    qn, kn, gc_col, a, rhs, qk = [], [], [], [], [], []
    for hh in heads:
        sl = slice(hh * dk, (hh + 1) * dk)
        q, k, v = qa[:, sl], ka[:, sl], va[:, sl]
        qn.append(q * lax.rsqrt(jnp.sum(q * q, axis=-1, keepdims=True) + RMS_EPS) * scale)
        kn.append(k * lax.rsqrt(jnp.sum(k * k, axis=-1, keepdims=True) + RMS_EPS))
        gc_col.append(gbl[:, hh:hh + 1])
        beta = gbl[:, hb + hh:hb + hh + 1]
        kb = kn[hh] * beta
        rhs.append(jnp.concatenate([v * beta, kb * jnp.exp(gc_col[hh])], axis=1).astype(BF16))
        for p in range(n_sub):
            rows = slice(p * sr, (p + 1) * sr)
            gc_row = gt[hh:hh + 1, rows]
            decay = jnp.where(causal, jnp.exp(jnp.where(causal, gc_col[hh][rows] - gc_row, 0.0)), 0.0)
            knb = kn[hh][rows].astype(BF16)
            a.append(jnp.where(strict, _nt(kb[rows].astype(BF16), knb) * decay, 0.0))
            qk.append(jnp.where(causal, _nt(qn[hh][rows].astype(BF16), knb) * decay, 0.0).astype(BF16))
    tinv = _unit_lower_inverses(a)
    solb = [jnp.concatenate([_dot(tinv[hh * n_sub + p].astype(BF16), rhs[hh][p * sr:(p + 1) * sr]).astype(BF16)
                             for p in range(n_sub)], axis=0) for hh in heads]
    qk_sol = [jnp.concatenate([_dot(qk[hh * n_sub + p], solb[hh][p * sr:(p + 1) * sr]) for p in range(n_sub)],
                              axis=0) for hh in heads]
    q_eff = [(qn[hh] * jnp.exp(gc_col[hh]) - qk_sol[hh][:, dk:]).astype(BF16) for hh in heads]
    g_last = [[gc_col[hh][(c + 1) * DN_CHUNK - 1:(c + 1) * DN_CHUNK, :] for c in range(n_chunks)] for hh in heads]
    kd_sol = [[_tn((kn[hh][c * DN_CHUNK:(c + 1) * DN_CHUNK]
                    * jnp.exp(g_last[hh][c] - gc_col[hh][c * DN_CHUNK:(c + 1) * DN_CHUNK])).astype(BF16),
                   solb[hh][c * DN_CHUNK:(c + 1) * DN_CHUNK])
               for c in range(n_chunks)] for hh in heads]
    s = [s_ref[hh] for hh in heads]
    outs = [[] for _ in heads]
    for c in range(n_chunks):
        lo, hi = c * DN_CHUNK, (c + 1) * DN_CHUNK
        for hh in heads:
            sb = s[hh].astype(BF16)
            outs[hh].append(qk_sol[hh][lo:hi, :dk] + _dot(q_eff[hh][lo:hi], sb))
            s[hh] = (s[hh] * jnp.exp(g_last[hh][c]) - _dot(kd_sol[hh][c][:, dk:].astype(BF16), sb)
                     + kd_sol[hh][c][:, :dk])
    for hh in heads:
        s_ref[hh] = s[hh]
        o = jnp.concatenate(outs[hh], axis=0)
        o = o * lax.rsqrt(jnp.mean(o * o, axis=-1, keepdims=True) + RMS_EPS) * ng_ref[...]
        o = o * jax.nn.silu(za[:, hh * dk:(hh + 1) * dk])
        o_ref[:, hh * dk:(hh + 1) * dk] = o.astype(o_ref.dtype)


def deltanet(qkvz, conv_w, gparams, norm_g, y, *, batch, n_heads):
    n = qkvz.shape[0]
    dk = norm_g.shape[0]
    hb = min(DN_HB, n_heads)
    w = hb * dk
    nhb = n_heads // hb
    seq = n // batch
    tb = min(DN_TB, seq)
    nt = seq // tb
    gc = gparams[:, :n_heads].reshape(n, nhb, hb)
    beta = gparams[:, n_heads:2 * n_heads].reshape(n, nhb, hb)
    cols = jnp.concatenate([gc, beta], axis=-1).transpose(1, 0, 2)
    cols = jnp.pad(cols, ((0, 0), (0, 0), (0, V7X_LANES - 2 * hb)))
    rows = jnp.pad(gc.transpose(1, 2, 0), ((0, 0), (0, V7X_SUBLANES - hb), (0, 0)))
    rmap = lambda part: (lambda b, h, t: (b * nt + t, part * nhb + h))
    cmap = lambda part: (lambda b, h, t: (0, part * nhb + h))
    taps = conv_w.shape[0]
    return pl.pallas_call(
        functools.partial(_deltanet_kernel, hb=hb, dk=dk),
        grid=(batch, nhb, nt),
        in_specs=[pl.BlockSpec((tb, w), rmap(0)), pl.BlockSpec((tb, w), rmap(1)),
                  pl.BlockSpec((tb, w), rmap(2)), pl.BlockSpec((tb, w), rmap(3)),
                  pl.BlockSpec((taps, w), cmap(0)), pl.BlockSpec((taps, w), cmap(1)),
                  pl.BlockSpec((taps, w), cmap(2)),
                  pl.BlockSpec((None, tb, V7X_LANES), lambda b, h, t: (h, b * nt + t, 0)),
                  pl.BlockSpec((None, V7X_SUBLANES, tb), lambda b, h, t: (h, 0, b * nt + t)),
                  pl.BlockSpec((1, dk), lambda b, h, t: (0, 0)),
                  pl.BlockSpec(memory_space=pl.ANY)],
        out_specs=pl.BlockSpec((tb, w), rmap(0)),
        out_shape=jax.ShapeDtypeStruct(y.shape, y.dtype),
        scratch_shapes=[pltpu.VMEM((hb, dk, dk), F32), pltpu.VMEM((3, V7X_SUBLANES, w), F32)],
        input_output_aliases={10: 0},
        compiler_params=_cparams(("parallel", "parallel", "arbitrary")),
        name="deltanet",
    )(qkvz, qkvz, qkvz, qkvz, conv_w, conv_w, conv_w, cols, rows, norm_g.reshape(1, dk), y)


POOL_HALO = 16


def _pool_kernel(x_ref, w_ref, b_ref, sc_ref, o_ref, carry_ref):
    t = pl.program_id(1)
    tt = x_ref.shape[0]
    cg = w_ref.shape[1]

    @pl.when(t == 0)
    def _():
        carry_ref[...] = jnp.zeros_like(carry_ref)

    x = x_ref[...].astype(F32)
    ext = jnp.concatenate([carry_ref[...], x], axis=0)
    carry_ref[...] = x[tt - POOL_HALO:tt]
    pos = t * tt + lax.broadcasted_iota(jnp.int32, (tt, 1), 0)
    for gi, win in enumerate(POOL_WINDOWS):
        cols = slice(gi * cg, (gi + 1) * cg)
        cur = ext[:, cols]
        span = 1
        while span < win:
            cur = cur[span:] + cur[:cur.shape[0] - span]
            span *= 2
        lo = POOL_HALO - (win - 1)
        total = cur[lo:lo + tt]
        count = jnp.minimum(pos + 1, win).astype(F32)
        pooled = total / count - x[:, cols]
        mixed = _dot(pooled.astype(BF16), w_ref[gi]) + b_ref[:, cols]
        o_ref[:, cols] = (mixed * sc_ref[:, cols]).astype(o_ref.dtype)


def pool(xp, pool_w, pool_b, pool_scale, *, batch, out_width):
    n, p = xp.shape
    assert out_width % p == 0 and max(POOL_WINDOWS) <= POOL_HALO
    seq = n // batch
    tt = min(POOL_TILE, seq)
    nt = seq // tt
    g, cg, _ = pool_w.shape
    const = lambda b, t: (0, 0)
    return pl.pallas_call(
        _pool_kernel,
        grid=(batch, nt),
        in_specs=[pl.BlockSpec((tt, p), lambda b, t: (b * nt + t, 0)),
                  pl.BlockSpec((g, cg, cg), lambda b, t: (0, 0, 0)),
                  pl.BlockSpec((1, p), const), pl.BlockSpec((1, p), const)],
        out_specs=pl.BlockSpec((tt, p), lambda b, t: (b * nt + t, out_width // p - 1)),
        out_shape=jax.ShapeDtypeStruct((n, out_width), BF16),
        scratch_shapes=[pltpu.VMEM((POOL_HALO, p), F32)],
        compiler_params=_cparams(("parallel", "arbitrary")),
        name="pool",
    )(xp, pool_w, pool_b.reshape(1, p).astype(F32), pool_scale.reshape(1, p).astype(F32))


def _sgu_kernel(u_ref, v_ref, g_ref, b_ref, ws_ref, bs_ref, o_ref):
    tt, width = u_ref.shape
    n_heads, chunk, _ = ws_ref.shape
    hd = width // n_heads
    vn = _layer_norm_rows(v_ref[...].astype(F32), g_ref[...], b_ref[...]).astype(BF16)
    ri = lax.broadcasted_iota(jnp.int32, (chunk, chunk), 0)
    ci = lax.broadcasted_iota(jnp.int32, (chunk, chunk), 1)
    for g in range(n_heads):
        cols = slice(g * hd, (g + 1) * hd)
        wsg = jnp.where(ri >= ci, ws_ref[g], 0.0).astype(BF16)
        bias = bs_ref[:, g:g + 1]
        for c in range(tt // chunk):
            rows = slice(c * chunk, (c + 1) * chunk)
            mixed = _dot(wsg, vn[rows, cols]) + bias
            o_ref[rows, cols] = (u_ref[rows, cols].astype(F32) * mixed).astype(o_ref.dtype)


def sgu(uv, ln_g, ln_b, w_s, b_s):
    n, two_w = uv.shape
    width = two_w // 2
    n_heads, chunk, _ = w_s.shape
    tt = min(SGU_TILE, n)
    assert tt % chunk == 0
    const = lambda i: (0, 0)
    return pl.pallas_call(
        _sgu_kernel,
        grid=(n // tt,),
        in_specs=[pl.BlockSpec((tt, width), lambda i: (i, 0)), pl.BlockSpec((tt, width), lambda i: (i, 1)),
                  pl.BlockSpec((1, width), const), pl.BlockSpec((1, width), const),
                  pl.BlockSpec((n_heads, chunk, chunk), lambda i: (0, 0, 0)),
                  pl.BlockSpec((chunk, n_heads), const)],
        out_specs=pl.BlockSpec((tt, width), lambda i: (i, 0)),
        out_shape=jax.ShapeDtypeStruct((n, width), BF16),
        compiler_params=_cparams(("parallel",)),
        name="sgu",
    )(uv, uv, ln_g.reshape(1, width), ln_b.reshape(1, width), w_s, jnp.transpose(b_s))


def even_mixer(hf, hb16, w_bf16, w_in_all, i, w_conv, a_log, dt_bias, norm_g, pool_w, pool_b, pool_scale,
               riders_in, riders_out, *, batch):
    w_qkvz, w_xp, w_out = w_bf16
    n_heads = a_log.shape[0]
    qkvz, rounded_in = matmul(hb16, w_qkvz, riders=riders_in)
    xp, _ = matmul(hb16, w_xp)
    gparams = gate_params(hf, w_in_all, i, w_qkvz.shape[1], a_log, dt_bias)
    y = pool(xp, pool_w.astype(BF16), pool_b, pool_scale, batch=batch, out_width=w_out.shape[0])
    y = deltanet(qkvz, w_conv, gparams, norm_g, y, batch=batch, n_heads=n_heads)
    mix, rounded_out = matmul(y, w_out, riders=riders_out)
    return mix, rounded_in, rounded_out


def odd_mixer(hb16, w_bf16, b_in, ln_g, ln_b, w_s, b_s, riders_in, riders_out):
    w_in, w_out = w_bf16
    uv, rounded_in = matmul(hb16, w_in, bias=b_in, act="gelu", riders=riders_in)
    gated = sgu(uv, ln_g, ln_b, w_s, b_s)
    mix, rounded_out = matmul(gated, w_out, riders=riders_out)
    return mix, rounded_in, rounded_out


def kernel(x, ev_w_in, ev_conv, ev_a_log, ev_dt_bias, ev_norm_g, ev_pool_w, ev_pool_b, ev_pool_scale, ev_w_out, od_w_in, od_b_in, od_ln_g, od_ln_b, od_w_s, od_b_s, od_w_out, ln_g, ln_b, router_w, router_bias, moe_w_gate, moe_w_up, moe_w_down):
    batch, seq, d = x.shape
    depth = ln_g.shape[0]
    alpha = (2.0 * depth) ** 0.25
    hf = x.reshape(batch * seq, d)
    router_wt = jnp.transpose(router_w)
    _, n_experts, _, f = moe_w_gate.shape
    wg_rows = moe_w_gate.reshape(depth * n_experts * d, f)
    wu_rows = moe_w_up.reshape(depth * n_experts * d, f)
    wd_rows = moe_w_down.reshape(depth * n_experts * f, d)
    o0 = 4 * ev_a_log.shape[1] * ev_norm_g.shape[1]
    o1 = o0 + 2 * ev_a_log.shape[1]
    stack = lambda w: w.reshape(-1, w.shape[-1])

    def mixer_riders(layer):
        j = layer // 2
        if layer % 2 == 0:
            return ((stack(ev_w_in), ev_w_in.shape[1], j, ((0, o0), (o1, ev_w_in.shape[2]))),
                    (stack(ev_w_out), ev_w_out.shape[1], j, None))
        return ((stack(od_w_in), od_w_in.shape[1], j, None), (stack(od_w_out), od_w_out.shape[1], j, None))

    *w_bf16, hb16 = round_weights(mixer_riders(0) + ((hf, hf.shape[0], 0, None),))
    for layer in range(depth):
        i = layer // 2
        riders = ((wg_rows, n_experts * d, layer, None), (wu_rows, n_experts * d, layer, None),
                  (wd_rows, n_experts * f, layer, None))
        next_riders = mixer_riders(layer + 1) if layer + 1 < depth else ()
        if layer % 2 == 0:
            mix, (wg, wu, wd), rounded = even_mixer(
                hf, hb16, w_bf16, ev_w_in, i, ev_conv[i], ev_a_log[i], ev_dt_bias[i], ev_norm_g[i],
                ev_pool_w[i], ev_pool_b[i], ev_pool_scale[i], riders, next_riders, batch=batch)
        else:
            mix, (wg, wu, wd), rounded = odd_mixer(hb16, w_bf16, od_b_in[i], od_ln_g[i], od_ln_b[i], od_w_s[i],
                                                   od_b_s[i], riders, next_riders)
        w_bf16 = rounded
        hf, hb16 = moe_block(hf, mix, ln_g[layer], ln_b[layer], router_wt, router_bias,
                             wg.reshape(n_experts, d, f), wu.reshape(n_experts, d, f),
                             wd.reshape(n_experts, f, d), alpha=alpha)
    return hf.reshape(batch, seq, d)
```

```python
import functools

import jax
import jax.numpy as jnp
from jax import lax
from jax.experimental import pallas as pl
from jax.experimental.pallas import tpu as pltpu

F32 = jnp.float32
BF16 = jnp.bfloat16

V7X_LANES = 128
V7X_SUBLANES = 8
V7X_VMEM_BYTES = 64 * 1024 * 1024
VMEM_LIMIT = V7X_VMEM_BYTES - 8 * 1024 * 1024

DN_CHUNK = 64
POOL_WINDOWS = (2, 4, 8, 16)
N_GROUPS = 4
TOP_K = 2
LN_EPS = 1e-5
RMS_EPS = 1e-6

MM_TM = 1024
MM_TN = 1024
TOK_TILE = 256
EXP_TILE = 512
DN_TB = 256
DN_SOLVE = 128
DN_HB = 8
ROUND_ROWS = 32
WAIT_UNROLL = 16
AB_TILE = 256
POOL_TILE = 256
SGU_TILE = 256


def _cparams(sem):
    return pltpu.CompilerParams(dimension_semantics=sem, vmem_limit_bytes=VMEM_LIMIT)


def _nt(a, b):
    return lax.dot_general(a, b, (((1,), (1,)), ((), ())), preferred_element_type=F32)


def _tn(a, b):
    return lax.dot_general(a, b, (((0,), (0,)), ((), ())), preferred_element_type=F32)


def _dot(a, b):
    return jnp.dot(a, b, preferred_element_type=F32)


def _split_bf16(x):
    hi = x.astype(BF16)
    return hi, (x - hi.astype(F32)).astype(BF16)


def _nt_split(a, b):
    a_hi, a_lo = _split_bf16(a)
    b_hi, b_lo = _split_bf16(b)
    return _nt(a_hi, b_hi) + (_nt(a_hi, b_lo) + _nt(a_lo, b_hi))


def _round_riders(rider_in, rider_out, rider_cols):
    outs = iter(rider_out)
    for src, ranges in zip(rider_in, rider_cols):
        for lo, hi in ranges:
            dst = next(outs)
            dst[...] = src[:, lo:hi].astype(dst.dtype)


def _rider_specs(riders, n_steps, step_index):
    in_specs, args, out_specs, out_shape, rider_cols = [], [], [], [], []
    for arr, rows, layer, ranges in riders:
        if arr.ndim == 3:
            cols = arr.shape[2]
            for lo, hi in ranges:
                rb = (hi - lo) // n_steps
                assert rb * n_steps == hi - lo and rb % (2 * V7X_SUBLANES) == 0 and lo % rb == 0
                in_specs.append(pl.BlockSpec((None, rb, cols), lambda *g, layer=layer, first=lo // rb:
                                             (layer, first + step_index(*g), 0)))
                args.append(arr)
                out_specs.append(pl.BlockSpec((rb, cols), lambda *g: (step_index(*g), 0)))
                out_shape.append(jax.ShapeDtypeStruct((hi - lo, cols), BF16))
                rider_cols.append(((0, cols),))
            continue
        cols = arr.shape[1]
        ranges = ((0, cols),) if ranges is None else ranges
        rb = rows // n_steps
        assert rb * n_steps == rows and rb % (2 * V7X_SUBLANES) == 0
        in_specs.append(pl.BlockSpec((rb, cols), lambda *g, layer=layer: (layer * n_steps + step_index(*g), 0)))
        args.append(arr)
        for lo, hi in ranges:
            out_specs.append(pl.BlockSpec((rb, hi - lo), lambda *g: (step_index(*g), 0)))
            out_shape.append(jax.ShapeDtypeStruct((rows, hi - lo), BF16))
        rider_cols.append(ranges)
    return in_specs, args, out_specs, out_shape, tuple(rider_cols)


def _mm_kernel(*refs, act, has_bias, rider_cols, w_is_transposed):
    n_in = 3 if has_bias else 2
    n_riders = len(rider_cols)
    x_ref, w_ref = refs[0], refs[1]
    o_ref = refs[n_in + n_riders]
    acc = _nt(x_ref[...], w_ref[...]) if w_is_transposed else _dot(x_ref[...], w_ref[...])
    if has_bias:
        acc = acc + refs[2][...]
    if act == "gelu":
        acc = jax.nn.gelu(acc)
    o_ref[...] = acc.astype(o_ref.dtype)
    _round_riders(refs[n_in:n_in + n_riders], refs[n_in + n_riders + 1:], rider_cols)


def matmul(x, w, bias=None, act=None, out_dtype=BF16, riders=(), w_is_transposed=False):
    m, k = x.shape
    n = w.shape[0] if w_is_transposed else w.shape[1]
    tm, tn = min(MM_TM, m), min(MM_TN, n)
    assert m % tm == 0 and n % tn == 0
    ni, nj = m // tm, n // tn
    w_spec = pl.BlockSpec((tn, k), lambda i, j: (j, 0)) if w_is_transposed else pl.BlockSpec((k, tn), lambda i, j: (0, j))
    in_specs = [pl.BlockSpec((tm, k), lambda i, j: (i, 0)), w_spec]
    args = [x, w]
    if bias is not None:
        in_specs.append(pl.BlockSpec((1, tn), lambda i, j: (0, j)))
        args.append(bias.reshape(1, n).astype(F32))
    r_in, r_args, r_out, r_shape, rider_cols = _rider_specs(riders, ni * nj, lambda i, j: i * nj + j)
    outs = pl.pallas_call(
        functools.partial(_mm_kernel, act=act, has_bias=bias is not None, rider_cols=rider_cols,
                          w_is_transposed=w_is_transposed),
        grid=(ni, nj),
        in_specs=in_specs + r_in,
        out_specs=[pl.BlockSpec((tm, tn), lambda i, j: (i, j))] + r_out,
        out_shape=[jax.ShapeDtypeStruct((m, n), out_dtype)] + r_shape,
        compiler_params=_cparams(("parallel", "arbitrary")),
        name="matmul",
    )(*args, *r_args)
    return outs[0], list(outs[1:])


def _round_kernel(*refs, rider_cols):
    n = len(rider_cols)
    _round_riders(refs[:n], refs[n:], rider_cols)


def round_weights(riders, n_steps):
    r_in, r_args, r_out, r_shape, rider_cols = _rider_specs(riders, n_steps, lambda s: s)
    return pl.pallas_call(
        functools.partial(_round_kernel, rider_cols=rider_cols),
        grid=(n_steps,),
        in_specs=r_in,
        out_specs=r_out,
        out_shape=r_shape,
        compiler_params=_cparams(("parallel",)),
        name="round_weights",
    )(*r_args)


def _pack_rows(x):
    half = x.shape[1] // 2
    bits = lambda v: lax.bitcast_convert_type(v.astype(BF16).astype(F32), jnp.uint32)
    return bits(x[:, :half]) | (bits(x[:, half:]) >> 16)


def _unpack_rows(words):
    hi = lax.bitcast_convert_type(words & jnp.uint32(0xFFFF0000), F32)
    lo = lax.bitcast_convert_type(words << 16, F32)
    return hi, lo


def _layer_norm_rows(x, g, b):
    mu = jnp.mean(x, axis=-1, keepdims=True)
    xc = x - mu
    var = jnp.mean(xc * xc, axis=-1, keepdims=True)
    return xc * lax.rsqrt(var + LN_EPS) * g + b


def _ln_router_kernel(h_ref, mix_ref, g_ref, b_ref, rwt_ref, rb_ref,
                      h1_ref, h1p_ref, dest_ref, gate_ref, cnt_ref, carry_ref,
                      *, alpha, n_experts, cap):
    i = pl.program_id(0)
    epg = n_experts // N_GROUPS
    tm = h_ref.shape[0]

    @pl.when(i == 0)
    def _():
        carry_ref[...] = jnp.zeros_like(carry_ref)

    x = alpha * h_ref[...] + mix_ref[...].astype(F32)
    y = _layer_norm_rows(x, g_ref[...], b_ref[...])
    h1_ref[...] = y
    h1p_ref[...] = _pack_rows(y)

    logits = _nt_split(rwt_ref[...], y)
    mx = jnp.max(logits, axis=0, keepdims=True)
    ex = jnp.exp(logits - mx)
    probs = ex / jnp.sum(ex, axis=0, keepdims=True)
    sel = probs + rb_ref[...]

    srow = [sel[e:e + 1, :] for e in range(n_experts)]
    prow = [probs[e:e + 1, :] for e in range(n_experts)]

    def top2_sum(vals):
        best = None
        for a in range(len(vals)):
            for c in range(a + 1, len(vals)):
                s = vals[a] + vals[c]
                best = s if best is None else jnp.maximum(best, s)
        return best

    gscore = [top2_sum(srow[g * epg:(g + 1) * epg]) for g in range(N_GROUPS)]
    best_g = jnp.zeros_like(gscore[0], dtype=jnp.int32)
    best_s = gscore[0]
    for g in range(1, N_GROUPS):
        upd = gscore[g] > best_s
        best_g = jnp.where(upd, g, best_g)
        best_s = jnp.where(upd, gscore[g], best_s)

    def pick(rows, j):
        out = rows[j]
        for g in range(1, N_GROUPS):
            out = jnp.where(best_g == g, rows[g * epg + j], out)
        return out

    in_s = [pick(srow, j) for j in range(epg)]
    in_p = [pick(prow, j) for j in range(epg)]

    def argmax_first(vals, exclude=None):
        bi = None
        bv = None
        for j, v in enumerate(vals):
            if exclude is not None:
                v = jnp.where(exclude == j, -jnp.inf, v)
            if bi is None:
                bi = jnp.zeros_like(best_g)
                bv = v
            else:
                upd = v > bv
                bi = jnp.where(upd, j, bi)
                bv = jnp.where(upd, v, bv)
        return bi

    loc0 = argmax_first(in_s)
    loc1 = argmax_first(in_s, exclude=loc0)

    def take(vals, idx):
        out = vals[0]
        for j in range(1, len(vals)):
            out = jnp.where(idx == j, vals[j], out)
        return out

    p0 = take(in_p, loc0)
    p1 = take(in_p, loc1)
    psum = p0 + p1
    gate_ref[...] = jnp.concatenate([p0 / psum, p1 / psum], axis=0)[None]

    idx0 = best_g * epg + loc0
    idx1 = best_g * epg + loc1

    eid = lax.broadcasted_iota(jnp.int32, (n_experts, tm), 0)
    hit0 = eid == idx0
    hit1 = eid == idx1
    member = jnp.where(hit0, 1.0, jnp.where(hit1, 1.0, 0.0))
    srci = lax.broadcasted_iota(jnp.int32, (tm, tm), 0)
    dsti = lax.broadcasted_iota(jnp.int32, (tm, tm), 1)
    upper = jnp.where(srci < dsti, 1.0, 0.0).astype(BF16)
    prefix = _dot(member.astype(BF16), upper)
    carry = carry_ref[...]
    pos = eid.astype(F32) * float(cap) + carry[:, :1] + prefix
    dest0 = jnp.sum(jnp.where(hit0, pos, 0.0), axis=0, keepdims=True)
    dest1 = jnp.sum(jnp.where(hit1, pos, 0.0), axis=0, keepdims=True)
    dest_ref[...] = jnp.concatenate([dest0, dest1], axis=0).astype(jnp.int32)[None]
    carry = carry + jnp.sum(member, axis=1, keepdims=True)
    carry_ref[...] = carry
    cnt_ref[...] = carry.astype(jnp.int32)


def ln_router(h, mix, ln_g, ln_b, router_wt, router_bias, *, alpha, cap):
    n, d = h.shape
    n_experts = router_wt.shape[0]
    tm = min(TOK_TILE, n)
    nb = n // tm
    row = lambda i: (i, 0)
    const = lambda i: (0, 0)
    return pl.pallas_call(
        functools.partial(_ln_router_kernel, alpha=alpha, n_experts=n_experts, cap=cap),
        grid=(nb,),
        in_specs=[pl.BlockSpec((tm, d), row), pl.BlockSpec((tm, d), row),
                  pl.BlockSpec((1, d), const), pl.BlockSpec((1, d), const),
                  pl.BlockSpec((n_experts, d), const), pl.BlockSpec((n_experts, 1), const)],
        out_specs=[pl.BlockSpec((tm, d), row), pl.BlockSpec((tm, d // 2), row),
                   pl.BlockSpec((1, TOP_K, tm), lambda i: (i, 0, 0)),
                   pl.BlockSpec((1, TOP_K, tm), lambda i: (i, 0, 0)),
                   pl.BlockSpec((n_experts, V7X_LANES), const)],
        out_shape=[jax.ShapeDtypeStruct((n, d), F32), jax.ShapeDtypeStruct((n, d // 2), jnp.uint32),
                   jax.ShapeDtypeStruct((nb, TOP_K, tm), jnp.int32),
                   jax.ShapeDtypeStruct((nb, TOP_K, tm), F32),
                   jax.ShapeDtypeStruct((n_experts, V7X_LANES), jnp.int32)],
        scratch_shapes=[pltpu.VMEM((n_experts, V7X_LANES), F32)],
        compiler_params=_cparams(("arbitrary",)),
        name="ln_router",
    )(h, mix, ln_g.reshape(1, d), ln_b.reshape(1, d), router_wt, router_bias.reshape(n_experts, 1))


def _row_copy(src_ref, src_row, dst_ref, dst_row, sem):
    return pltpu.make_async_copy(src_ref.at[pl.ds(src_row, 1), :], dst_ref.at[pl.ds(dst_row, 1), :], sem)


def _dispatch_kernel(cnt_ref, dest_ref, x_ref, xs_ref, zero_ref, sem, zsem, *, cap, n_experts):
    i = pl.program_id(0)
    tm = x_ref.shape[0]

    for r in range(tm):
        for k in range(TOP_K):
            _row_copy(x_ref, r, xs_ref, dest_ref[k, r], sem).start(priority=k)

    @pl.when(i == pl.num_programs(0) - 1)
    def _():
        zero_ref[...] = jnp.zeros_like(zero_ref)

        def pad_rows(e, do):
            cnt = cnt_ref[e]
            n_pad = lax.rem(EXP_TILE - lax.rem(cnt, EXP_TILE), EXP_TILE)

            def one(r, c):
                do(_row_copy(zero_ref, 0, xs_ref, e * cap + cnt + r, zsem))
                return c

            lax.fori_loop(0, n_pad, one, 0)
            return do

        lax.fori_loop(0, n_experts, lambda e, c: (pad_rows(e, lambda cp: cp.start()), c)[1], 0)
        lax.fori_loop(0, n_experts, lambda e, c: (pad_rows(e, lambda cp: cp.wait()), c)[1], 0)

    def wait(r, c):
        for k in range(TOP_K):
            _row_copy(x_ref, 0, xs_ref, 0, sem).wait()
        return c

    lax.fori_loop(0, tm, wait, 0, unroll=WAIT_UNROLL)


def dispatch(x, dest, counts, *, cap):
    n, d = x.shape
    n_experts = counts.shape[0]
    nb, _, tm = dest.shape
    return pl.pallas_call(
        functools.partial(_dispatch_kernel, cap=cap, n_experts=n_experts),
        grid_spec=pltpu.PrefetchScalarGridSpec(
            num_scalar_prefetch=1,
            grid=(nb,),
            in_specs=[pl.BlockSpec((None, TOP_K, tm), lambda i, c: (i, 0, 0), memory_space=pltpu.SMEM),
                      pl.BlockSpec((tm, d), lambda i, c: (i, 0))],
            out_specs=pl.BlockSpec(memory_space=pl.ANY),
            scratch_shapes=[pltpu.VMEM((V7X_SUBLANES, d), x.dtype),
                            pltpu.SemaphoreType.DMA(()), pltpu.SemaphoreType.DMA(())],
        ),
        out_shape=jax.ShapeDtypeStruct((n_experts * cap, d), x.dtype),
        compiler_params=_cparams(("arbitrary",)),
        name="dispatch",
    )(counts, dest, x)


def _expert_kernel(blk_ref, exp_ref, valid_ref, x_ref, wg_ref, wu_ref, wd_ref, o_ref):
    s = pl.program_id(0)

    @pl.when(valid_ref[s] == 1)
    def _():
        hi, lo = _unpack_rows(x_ref[...])
        x = jnp.concatenate([hi.astype(BF16), lo.astype(BF16)], axis=1)
        gate = _dot(x, wg_ref[...])
        up = _dot(x, wu_ref[...])
        act = (jax.nn.silu(gate) * up).astype(BF16)
        o_ref[...] = _pack_rows(_dot(act, wd_ref[...]))


def expert_ffn(xs, w_gate, w_up, w_down, counts, *, cap, n_tokens):
    d = xs.shape[1]
    n_experts, f = w_gate.shape[0], w_gate.shape[2]
    tm = EXP_TILE
    n_tiles = (TOP_K * n_tokens) // tm + n_experts
    tiles_e = (counts + tm - 1) // tm
    cum = jnp.cumsum(tiles_e)
    total = cum[-1]
    slot = jnp.minimum(jnp.arange(n_tiles, dtype=jnp.int32), total - 1)
    e_of = jnp.sum((slot[:, None] >= cum[None, :]).astype(jnp.int32), axis=1)
    first = jnp.take(cum - tiles_e, e_of)
    blk = (e_of * (cap // tm) + slot - first).astype(jnp.int32)
    valid = (jnp.arange(n_tiles) < total).astype(jnp.int32)
    xmap = lambda s, blk, ex, va: (blk[s], 0)
    return pl.pallas_call(
        _expert_kernel,
        grid_spec=pltpu.PrefetchScalarGridSpec(
            num_scalar_prefetch=3,
            grid=(n_tiles,),
            in_specs=[pl.BlockSpec((tm, d), xmap),
                      pl.BlockSpec((None, 2 * d, f), lambda s, blk, ex, va: (ex[s], 0, 0)),
                      pl.BlockSpec((None, 2 * d, f), lambda s, blk, ex, va: (ex[s], 0, 0)),
                      pl.BlockSpec((None, f, 2 * d), lambda s, blk, ex, va: (ex[s], 0, 0))],
            out_specs=pl.BlockSpec((tm, d), xmap),
        ),
        out_shape=jax.ShapeDtypeStruct(xs.shape, jnp.uint32),
        compiler_params=_cparams(("arbitrary",)),
        name="expert_ffn",
    )(blk, e_of.astype(jnp.int32), valid, xs, w_gate, w_up, w_down)


def _combine_kernel(dest_ref, next_dest_ref, h_ref, gate_ref, g_ref, b_ref, ys_ref, o_ref, ob_ref, buf_ref, sems,
                    *, alpha):
    i = pl.program_id(0)
    last = pl.num_programs(0) - 1
    tm = h_ref.shape[0]
    parity = lax.rem(i, 2)

    def gather(idx_ref, to_slot):
        for r in range(tm):
            for k in range(TOP_K):
                _row_copy(ys_ref, idx_ref[k, r], buf_ref.at[to_slot, k], r, sems.at[to_slot]).start(priority=k)

    def wait_slot(slot):
        def wait(r, c):
            for k in range(TOP_K):
                _row_copy(ys_ref, 0, buf_ref.at[slot, k], 0, sems.at[slot]).wait()
            return c

        lax.fori_loop(0, tm, wait, 0, unroll=WAIT_UNROLL)

    @pl.when(i == 0)
    def _():
        gather(dest_ref, 0)

    def step(cur):
        wait_slot(cur)
        gather(next_dest_ref, 1 - cur)
        gates = gate_ref[...]
        hi0, lo0 = _unpack_rows(buf_ref[cur, 0])
        hi1, lo1 = _unpack_rows(buf_ref[cur, 1])
        ffn = jnp.concatenate([gates[:, 0:1] * hi0 + gates[:, 1:2] * hi1,
                               gates[:, 0:1] * lo0 + gates[:, 1:2] * lo1], axis=1)
        x = alpha * h_ref[...] + ffn
        y = _layer_norm_rows(x, g_ref[...], b_ref[...])
        o_ref[...] = y
        ob_ref[...] = y.astype(BF16)

        @pl.when(i == last)
        def _():
            wait_slot(1 - cur)

    for cur in range(2):
        pl.when(parity == cur)(functools.partial(step, cur))


def combine(h, dest, gates_cols, ln_g, ln_b, ys, *, alpha):
    n, d = h.shape
    nb, _, tm = dest.shape
    row = lambda i: (i, 0)
    const = lambda i: (0, 0)
    return pl.pallas_call(
        functools.partial(_combine_kernel, alpha=alpha),
        grid=(nb,),
        in_specs=[pl.BlockSpec((None, TOP_K, tm), lambda i: (i, 0, 0), memory_space=pltpu.SMEM),
                  pl.BlockSpec((None, TOP_K, tm), lambda i: (jnp.minimum(i + 1, nb - 1), 0, 0),
                               memory_space=pltpu.SMEM),
                  pl.BlockSpec((tm, d), row), pl.BlockSpec((tm, TOP_K), row),
                  pl.BlockSpec((1, d), const), pl.BlockSpec((1, d), const),
                  pl.BlockSpec(memory_space=pl.ANY)],
        out_specs=[pl.BlockSpec((tm, d), row), pl.BlockSpec((tm, d), row)],
        out_shape=[jax.ShapeDtypeStruct((n, d), F32), jax.ShapeDtypeStruct((n, d), BF16)],
        scratch_shapes=[pltpu.VMEM((2, TOP_K, tm, d // 2), jnp.uint32), pltpu.SemaphoreType.DMA((2,))],
        compiler_params=_cparams(("arbitrary",)),
        name="combine",
    )(dest, dest, h, gates_cols, ln_g.reshape(1, d), ln_b.reshape(1, d), ys)


def moe_block(h, mix, ln_g, ln_b, router_wt, router_bias, w_gate, w_up, w_down, *, alpha):
    n, d = h.shape
    cap = n + EXP_TILE
    h1, h1p, dest, gates, cnt = ln_router(h, mix, ln_g[0], ln_b[0], router_wt, router_bias,
                                          alpha=alpha, cap=cap)
    counts = cnt[:, 0]
    xs = dispatch(h1p, dest, counts, cap=cap)
    ys = expert_ffn(xs, w_gate, w_up, w_down, counts, cap=cap, n_tokens=n)
    gates_cols = jnp.transpose(gates, (0, 2, 1)).reshape(n, TOP_K)
    return combine(h1, dest, gates_cols, ln_g[1], ln_b[1], ys, alpha=alpha)


def _chunk_masks(n, chunk):
    ri = lax.broadcasted_iota(jnp.int32, (n, n), 0)
    ci = lax.broadcasted_iota(jnp.int32, (n, n), 1)
    same = (ri // chunk) == (ci // chunk)
    return same & (ri >= ci), same & (ri > ci)


def _gate_params_kernel(h_ref, w_ref, alog_ref, dtb_ref, o_ref, *, n_heads):
    tm = h_ref.shape[0]
    h_hi, h_lo = _split_bf16(h_ref[...])
    w_hi, w_lo = _split_bf16(w_ref[...])
    ab = _nt(h_hi, w_hi) + (_nt(h_hi, w_lo) + _nt(h_lo, w_hi))
    g = -jnp.exp(alog_ref[...]) * jax.nn.softplus(ab + dtb_ref[...])
    beta = jax.nn.sigmoid(ab)
    causal, _ = _chunk_masks(tm, DN_CHUNK)
    tri = jnp.where(causal, 1.0, 0.0).astype(BF16)
    g_hi, g_rest = _split_bf16(g)
    g_mid, g_lo = _split_bf16(g - g_hi.astype(F32))
    del g_rest
    gc = _dot(tri, g_hi) + (_dot(tri, g_mid) + _dot(tri, g_lo))
    lane = lax.broadcasted_iota(jnp.int32, ab.shape, 1)
    o_ref[...] = jnp.where(lane < n_heads, gc, beta)


def gate_params(h, w_in_t_all, layer, col0, a_log, dt_bias):
    n, d = h.shape
    n_heads = a_log.shape[0]
    assert col0 % V7X_LANES == 0 and 2 * n_heads <= V7X_LANES
    tm = min(AB_TILE, n)
    lanes = lambda v: jnp.pad(v.astype(F32), (0, V7X_LANES - n_heads)).reshape(1, V7X_LANES)
    const = lambda i: (0, 0)
    return pl.pallas_call(
        functools.partial(_gate_params_kernel, n_heads=n_heads),
        grid=(n // tm,),
        in_specs=[pl.BlockSpec((tm, d), lambda i: (i, 0)),
                  pl.BlockSpec((None, V7X_LANES, d), lambda i: (layer, col0 // V7X_LANES, 0)),
                  pl.BlockSpec((1, V7X_LANES), const), pl.BlockSpec((1, V7X_LANES), const)],
        out_specs=pl.BlockSpec((tm, V7X_LANES), lambda i: (i, 0)),
        out_shape=jax.ShapeDtypeStruct((n, V7X_LANES), F32),
        compiler_params=_cparams(("parallel",)),
        name="gate_params",
    )(h, w_in_t_all, lanes(a_log), lanes(dt_bias))


def _unit_lower_inverses(mats):
    n = mats[0].shape[0]
    ri = lax.broadcasted_iota(jnp.int32, (n, n), 0)
    ci = lax.broadcasted_iota(jnp.int32, (n, n), 1)
    differ = ri ^ ci
    eye = jnp.where(ri == ci, 1.0, 0.0)
    base = (differ >> 1) == 0
    invs = [eye - jnp.where(base, a, 0.0) for a in mats]
    level = 1
    while (1 << level) < DN_CHUNK:
        at_level = (differ >> level) == 1
        invbs = [inv.astype(BF16) for inv in invs]
        halves = [_dot(jnp.where(at_level, a, 0.0).astype(BF16), invb).astype(BF16)
                  for a, invb in zip(mats, invbs)]
        invs = [inv - _dot(invb, half) for inv, invb, half in zip(invs, invbs, halves)]
        level += 1
    return invs


def _deltanet_kernel(q_ref, k_ref, v_ref, z_ref, cq_ref, ck_ref, cv_ref, gb_ref, gt_ref, ng_ref, y_any,
                     o_ref, s_ref, carry_ref, *, hb, dk):
    del y_any
    t = pl.program_id(2)
    tb = q_ref.shape[0]
    scale = dk ** -0.5

    @pl.when(t == 0)
    def _():
        s_ref[...] = jnp.zeros_like(s_ref)
        carry_ref[...] = jnp.zeros_like(carry_ref)

    def conv_silu(x_ref, w_ref, slot):
        raw = x_ref[...].astype(F32)
        ext = jnp.concatenate([carry_ref[slot], raw], axis=0)
        w = w_ref[...]
        taps = w.shape[0]
        acc = None
        for j in range(taps):
            lag = taps - 1 - j
            window = pltpu.roll(ext, lag, axis=0)[V7X_SUBLANES:] if lag else raw
            term = window * w[j:j + 1, :]
            acc = term if acc is None else acc + term
        carry_ref[slot] = raw[tb - V7X_SUBLANES:tb]
        return jax.nn.silu(acc)

    qa = conv_silu(q_ref, cq_ref, 0)
    ka = conv_silu(k_ref, ck_ref, 1)
    va = conv_silu(v_ref, cv_ref, 2)
    za = z_ref[...].astype(F32)
    gbl = gb_ref[...]
    gt = gt_ref[...]
    sr = min(DN_SOLVE, tb)
    n_sub = tb // sr
    causal, strict = _chunk_masks(sr, DN_CHUNK)

    heads = range(hb)
    n_chunks = tb // DN_CHUNK
    qn, kn, gc_col, a, rhs, qk = [], [], [], [], [], []
    for hh in heads:
        sl = slice(hh * dk, (hh + 1) * dk)
        q, k, v = qa[:, sl], ka[:, sl], va[:, sl]
        qn.append(q * lax.rsqrt(jnp.sum(q * q, axis=-1, keepdims=True) + RMS_EPS) * scale)
        kn.append(k * lax.rsqrt(jnp.sum(k * k, axis=-1, keepdims=True) + RMS_EPS))
        gc_col.append(gbl[:, hh:hh + 1])
        beta = gbl[:, hb + hh:hb + hh + 1]
        kb = kn[hh] * beta
        rhs.append(jnp.concatenate([v * beta, kb * jnp.exp(gc_col[hh])], axis=1).astype(BF16))
        for p in range(n_sub):
            rows = slice(p * sr, (p + 1) * sr)
            gc_row = gt[hh:hh + 1, rows]
            decay = jnp.where(causal, jnp.exp(jnp.where(causal, gc_col[hh][rows] - gc_row, 0.0)), 0.0)
            knb = kn[hh][rows].astype(BF16)
            a.append(jnp.where(strict, _nt(kb[rows].astype(BF16), knb) * decay, 0.0))
            qk.append(jnp.where(causal, _nt(qn[hh][rows].astype(BF16), knb) * decay, 0.0).astype(BF16))
    tinv = _unit_lower_inverses(a)
    solb = [jnp.concatenate([_dot(tinv[hh * n_sub + p].astype(BF16), rhs[hh][p * sr:(p + 1) * sr]).astype(BF16)
                             for p in range(n_sub)], axis=0) for hh in heads]
    qk_sol = [jnp.concatenate([_dot(qk[hh * n_sub + p], solb[hh][p * sr:(p + 1) * sr]) for p in range(n_sub)],
                              axis=0) for hh in heads]
    q_eff = [(qn[hh] * jnp.exp(gc_col[hh]) - qk_sol[hh][:, dk:]).astype(BF16) for hh in heads]
    g_last = [[gc_col[hh][(c + 1) * DN_CHUNK - 1:(c + 1) * DN_CHUNK, :] for c in range(n_chunks)] for hh in heads]
    kd_sol = [[_tn((kn[hh][c * DN_CHUNK:(c + 1) * DN_CHUNK]
                    * jnp.exp(g_last[hh][c] - gc_col[hh][c * DN_CHUNK:(c + 1) * DN_CHUNK])).astype(BF16),
                   solb[hh][c * DN_CHUNK:(c + 1) * DN_CHUNK])
               for c in range(n_chunks)] for hh in heads]
    s = [s_ref[hh] for hh in heads]
    outs = [[] for _ in heads]
    for c in range(n_chunks):
        lo, hi = c * DN_CHUNK, (c + 1) * DN_CHUNK
        for hh in heads:
            sb = s[hh].astype(BF16)
            outs[hh].append(qk_sol[hh][lo:hi, :dk] + _dot(q_eff[hh][lo:hi], sb))
            s[hh] = (s[hh] * jnp.exp(g_last[hh][c]) - _dot(kd_sol[hh][c][:, dk:].astype(BF16), sb)
                     + kd_sol[hh][c][:, :dk])
    for hh in heads:
        s_ref[hh] = s[hh]
        o = jnp.concatenate(outs[hh], axis=0)
        o = o * lax.rsqrt(jnp.mean(o * o, axis=-1, keepdims=True) + RMS_EPS) * ng_ref[...]
        o = o * jax.nn.silu(za[:, hh * dk:(hh + 1) * dk])
        o_ref[:, hh * dk:(hh + 1) * dk] = o.astype(o_ref.dtype)


def deltanet(qkvz, conv_w, gparams, norm_g, y, *, batch, n_heads):
    n = qkvz.shape[0]
    dk = norm_g.shape[0]
    hb = min(DN_HB, n_heads)
    w = hb * dk
    nhb = n_heads // hb
    seq = n // batch
    tb = min(DN_TB, seq)
    nt = seq // tb
    gc = gparams[:, :n_heads].reshape(n, nhb, hb)
    beta = gparams[:, n_heads:2 * n_heads].reshape(n, nhb, hb)
    cols = jnp.concatenate([gc, beta], axis=-1).transpose(1, 0, 2)
    cols = jnp.pad(cols, ((0, 0), (0, 0), (0, V7X_LANES - 2 * hb)))
    rows = jnp.pad(gc.transpose(1, 2, 0), ((0, 0), (0, V7X_SUBLANES - hb), (0, 0)))
    rmap = lambda part: (lambda b, h, t: (b * nt + t, part * nhb + h))
    cmap = lambda part: (lambda b, h, t: (0, part * nhb + h))
    taps = conv_w.shape[0]
    return pl.pallas_call(
        functools.partial(_deltanet_kernel, hb=hb, dk=dk),
        grid=(batch, nhb, nt),
        in_specs=[pl.BlockSpec((tb, w), rmap(0)), pl.BlockSpec((tb, w), rmap(1)),
                  pl.BlockSpec((tb, w), rmap(2)), pl.BlockSpec((tb, w), rmap(3)),
                  pl.BlockSpec((taps, w), cmap(0)), pl.BlockSpec((taps, w), cmap(1)),
                  pl.BlockSpec((taps, w), cmap(2)),
                  pl.BlockSpec((None, tb, V7X_LANES), lambda b, h, t: (h, b * nt + t, 0)),
                  pl.BlockSpec((None, V7X_SUBLANES, tb), lambda b, h, t: (h, 0, b * nt + t)),
                  pl.BlockSpec((1, dk), lambda b, h, t: (0, 0)),
                  pl.BlockSpec(memory_space=pl.ANY)],
        out_specs=pl.BlockSpec((tb, w), rmap(0)),
        out_shape=jax.ShapeDtypeStruct(y.shape, y.dtype),
        scratch_shapes=[pltpu.VMEM((hb, dk, dk), F32), pltpu.VMEM((3, V7X_SUBLANES, w), F32)],
        input_output_aliases={10: 0},
        compiler_params=_cparams(("parallel", "parallel", "arbitrary")),
        name="deltanet",
    )(qkvz, qkvz, qkvz, qkvz, conv_w, conv_w, conv_w, cols, rows, norm_g.reshape(1, dk), y)


POOL_HALO = 16


def _pool_kernel(x_ref, w_ref, b_ref, sc_ref, o_ref, carry_ref):
    t = pl.program_id(1)
    tt = x_ref.shape[0]
    cg = w_ref.shape[1]

    @pl.when(t == 0)
    def _():
        carry_ref[...] = jnp.zeros_like(carry_ref)

    x = x_ref[...].astype(F32)
    ext = jnp.concatenate([carry_ref[...], x], axis=0)
    carry_ref[...] = x[tt - POOL_HALO:tt]
    pos = t * tt + lax.broadcasted_iota(jnp.int32, (tt, 1), 0)
    for gi, win in enumerate(POOL_WINDOWS):
        cols = slice(gi * cg, (gi + 1) * cg)
        cur = ext[:, cols]
        span = 1
        while span < win:
            cur = cur[span:] + cur[:cur.shape[0] - span]
            span *= 2
        lo = POOL_HALO - (win - 1)
        total = cur[lo:lo + tt]
        count = jnp.minimum(pos + 1, win).astype(F32)
        pooled = total / count - x[:, cols]
        mixed = _dot(pooled.astype(BF16), w_ref[gi]) + b_ref[:, cols]
        o_ref[:, cols] = (mixed * sc_ref[:, cols]).astype(o_ref.dtype)


def pool(xp, pool_w, pool_b, pool_scale, *, batch, out_width):
    n, p = xp.shape
    assert out_width % p == 0 and max(POOL_WINDOWS) <= POOL_HALO
    seq = n // batch
    tt = min(POOL_TILE, seq)
    nt = seq // tt
    g, cg, _ = pool_w.shape
    const = lambda b, t: (0, 0)
    return pl.pallas_call(
        _pool_kernel,
        grid=(batch, nt),
        in_specs=[pl.BlockSpec((tt, p), lambda b, t: (b * nt + t, 0)),
                  pl.BlockSpec((g, cg, cg), lambda b, t: (0, 0, 0)),
                  pl.BlockSpec((1, p), const), pl.BlockSpec((1, p), const)],
        out_specs=pl.BlockSpec((tt, p), lambda b, t: (b * nt + t, out_width // p - 1)),
        out_shape=jax.ShapeDtypeStruct((n, out_width), BF16),
        scratch_shapes=[pltpu.VMEM((POOL_HALO, p), F32)],
        compiler_params=_cparams(("parallel", "arbitrary")),
        name="pool",
    )(xp, pool_w, pool_b.reshape(1, p).astype(F32), pool_scale.reshape(1, p).astype(F32))


def _sgu_kernel(u_ref, v_ref, g_ref, b_ref, ws_ref, bs_ref, o_ref):
    tt, width = u_ref.shape
    n_heads, chunk, _ = ws_ref.shape
    hd = width // n_heads
    vn = _layer_norm_rows(v_ref[...].astype(F32), g_ref[...], b_ref[...]).astype(BF16)
    ri = lax.broadcasted_iota(jnp.int32, (chunk, chunk), 0)
    ci = lax.broadcasted_iota(jnp.int32, (chunk, chunk), 1)
    for g in range(n_heads):
        cols = slice(g * hd, (g + 1) * hd)
        wsg = jnp.where(ri >= ci, ws_ref[g], 0.0).astype(BF16)
        bias = bs_ref[:, g:g + 1]
        for c in range(tt // chunk):
            rows = slice(c * chunk, (c + 1) * chunk)
            mixed = _dot(wsg, vn[rows, cols]) + bias
            o_ref[rows, cols] = (u_ref[rows, cols].astype(F32) * mixed).astype(o_ref.dtype)


def sgu(uv, ln_g, ln_b, w_s, b_s):
    n, two_w = uv.shape
    width = two_w // 2
    n_heads, chunk, _ = w_s.shape
    tt = min(SGU_TILE, n)
    assert tt % chunk == 0
    const = lambda i: (0, 0)
    return pl.pallas_call(
        _sgu_kernel,
        grid=(n // tt,),
        in_specs=[pl.BlockSpec((tt, width), lambda i: (i, 0)), pl.BlockSpec((tt, width), lambda i: (i, 1)),
                  pl.BlockSpec((1, width), const), pl.BlockSpec((1, width), const),
                  pl.BlockSpec((n_heads, chunk, chunk), lambda i: (0, 0, 0)),
                  pl.BlockSpec((chunk, n_heads), const)],
        out_specs=pl.BlockSpec((tt, width), lambda i: (i, 0)),
        out_shape=jax.ShapeDtypeStruct((n, width), BF16),
        compiler_params=_cparams(("parallel",)),
        name="sgu",
    )(uv, uv, ln_g.reshape(1, width), ln_b.reshape(1, width), w_s, jnp.transpose(b_s))


def even_mixer(hf, hb16, w_bf16, w_in_t_all, i, w_conv, a_log, dt_bias, norm_g, pool_w, pool_b, pool_scale,
               riders_in, riders_out, *, batch):
    w_qkvz_t, w_xp_t, w_out = w_bf16
    n_heads = a_log.shape[0]
    qkvz, rounded_in = matmul(hb16, w_qkvz_t, riders=riders_in, w_is_transposed=True)
    xp, _ = matmul(hb16, w_xp_t, w_is_transposed=True)
    gparams = gate_params(hf, w_in_t_all, i, w_qkvz_t.shape[0], a_log, dt_bias)
    y = pool(xp, pool_w.astype(BF16), pool_b, pool_scale, batch=batch, out_width=w_out.shape[0])
    y = deltanet(qkvz, w_conv, gparams, norm_g, y, batch=batch, n_heads=n_heads)
    mix, rounded_out = matmul(y, w_out, riders=riders_out)
    return mix, rounded_in, rounded_out


def odd_mixer(hb16, w_bf16, b_in, ln_g, ln_b, w_s, b_s, riders_in, riders_out):
    w_in, w_out = w_bf16
    uv, rounded_in = matmul(hb16, w_in, bias=b_in, act="gelu", riders=riders_in)
    gated = sgu(uv, ln_g, ln_b, w_s, b_s)
    mix, rounded_out = matmul(gated, w_out, riders=riders_out)
    return mix, rounded_in, rounded_out


def kernel(x, ev_w_in, ev_conv, ev_a_log, ev_dt_bias, ev_norm_g, ev_pool_w, ev_pool_b, ev_pool_scale, ev_w_out, od_w_in, od_b_in, od_ln_g, od_ln_b, od_w_s, od_b_s, od_w_out, ln_g, ln_b, router_w, router_bias, moe_w_gate, moe_w_up, moe_w_down):
    batch, seq, d = x.shape
    depth = ln_g.shape[0]
    alpha = (2.0 * depth) ** 0.25
    hf = x.reshape(batch * seq, d)
    router_wt = jnp.transpose(router_w)
    _, n_experts, _, f = moe_w_gate.shape
    wg_rows = moe_w_gate.reshape(depth * n_experts * d, f)
    wu_rows = moe_w_up.reshape(depth * n_experts * d, f)
    wd_rows = moe_w_down.reshape(depth * n_experts * f, d)
    o0 = 4 * ev_a_log.shape[1] * ev_norm_g.shape[1]
    o1 = o0 + 2 * ev_a_log.shape[1]
    ev_w_in_t = jnp.swapaxes(ev_w_in, 1, 2)
    stack = lambda w: w.reshape(-1, w.shape[-1])

    def mixer_riders(layer):
        j = layer // 2
        if layer % 2 == 0:
            return ((ev_w_in_t, None, j, ((0, o0), (o1, ev_w_in_t.shape[1]))),
                    (stack(ev_w_out), ev_w_out.shape[1], j, None))
        return ((stack(od_w_in), od_w_in.shape[1], j, None), (stack(od_w_out), od_w_out.shape[1], j, None))

    n_round = max(1, min(ev_w_out.shape[1], ev_w_in_t.shape[1] - o1) // ROUND_ROWS)
    *w_bf16, hb16 = round_weights(mixer_riders(0) + ((hf, hf.shape[0], 0, None),), n_round)
    for layer in range(depth):
        i = layer // 2
        riders = ((wg_rows, n_experts * d, layer, None), (wu_rows, n_experts * d, layer, None),
                  (wd_rows, n_experts * f, layer, None))
        next_riders = mixer_riders(layer + 1) if layer + 1 < depth else ()
        if layer % 2 == 0:
            mix, (wg, wu, wd), rounded = even_mixer(
                hf, hb16, w_bf16, ev_w_in_t, i, ev_conv[i], ev_a_log[i], ev_dt_bias[i], ev_norm_g[i],
                ev_pool_w[i], ev_pool_b[i], ev_pool_scale[i], riders, next_riders, batch=batch)
        else:
            mix, (wg, wu, wd), rounded = odd_mixer(hb16, w_bf16, od_b_in[i], od_ln_g[i], od_ln_b[i], od_w_s[i],
                                                   od_b_s[i], riders, next_riders)
        w_bf16 = rounded
        hf, hb16 = moe_block(hf, mix, ln_g[layer], ln_b[layer], router_wt, router_bias,
                             wg.reshape(n_experts, d, f), wu.reshape(n_experts, d, f),
                             wd.reshape(n_experts, f, d), alpha=alpha)
    return hf.reshape(batch, seq, d)
```

```python
import functools

import jax
import jax.numpy as jnp
from jax import lax
from jax.experimental import pallas as pl
from jax.experimental.pallas import tpu as pltpu

F32 = jnp.float32
BF16 = jnp.bfloat16

V7X_LANES = 128
V7X_SUBLANES = 8
V7X_VMEM_BYTES = 64 * 1024 * 1024
VMEM_LIMIT = V7X_VMEM_BYTES - 8 * 1024 * 1024

DN_CHUNK = 64
POOL_WINDOWS = (2, 4, 8, 16)
N_GROUPS = 4
TOP_K = 2
LN_EPS = 1e-5
RMS_EPS = 1e-6

MM_TM = 1024
MM_TN = 1024
TOK_TILE = 256
EXP_TILE = 512
DN_TB = 256
DN_SOLVE = 128
DN_HB = 8
ROUND_ROWS = 32
WAIT_UNROLL = 16
AB_TILE = 256
POOL_TILE = 512
SGU_TILE = 512


def _cparams(sem):
    return pltpu.CompilerParams(dimension_semantics=sem, vmem_limit_bytes=VMEM_LIMIT)


def _nt(a, b):
    return lax.dot_general(a, b, (((1,), (1,)), ((), ())), preferred_element_type=F32)


def _tn(a, b):
    return lax.dot_general(a, b, (((0,), (0,)), ((), ())), preferred_element_type=F32)


def _dot(a, b):
    return jnp.dot(a, b, preferred_element_type=F32)


def _split_bf16(x):
    hi = x.astype(BF16)
    return hi, (x - hi.astype(F32)).astype(BF16)


def _nt_split(a, b):
    a_hi, a_lo = _split_bf16(a)
    b_hi, b_lo = _split_bf16(b)
    return _nt(a_hi, b_hi) + (_nt(a_hi, b_lo) + _nt(a_lo, b_hi))


def _round_riders(rider_in, rider_out, rider_cols):
    outs = iter(rider_out)
    for src, ranges in zip(rider_in, rider_cols):
        for lo, hi in ranges:
            dst = next(outs)
            dst[...] = src[:, lo:hi].astype(dst.dtype)


def _rider_specs(riders, n_steps, step_index):
    in_specs, args, out_specs, out_shape, rider_cols = [], [], [], [], []
    for arr, rows, layer, ranges in riders:
        if arr.ndim == 3:
            cols = arr.shape[2]
            for lo, hi in ranges:
                rb = (hi - lo) // n_steps
                assert rb * n_steps == hi - lo and rb % (2 * V7X_SUBLANES) == 0 and lo % rb == 0
                in_specs.append(pl.BlockSpec((None, rb, cols), lambda *g, layer=layer, first=lo // rb:
                                             (layer, first + step_index(*g), 0)))
                args.append(arr)
                out_specs.append(pl.BlockSpec((rb, cols), lambda *g: (step_index(*g), 0)))
                out_shape.append(jax.ShapeDtypeStruct((hi - lo, cols), BF16))
                rider_cols.append(((0, cols),))
            continue
        cols = arr.shape[1]
        ranges = ((0, cols),) if ranges is None else ranges
        rb = rows // n_steps
        assert rb * n_steps == rows and rb % (2 * V7X_SUBLANES) == 0
        in_specs.append(pl.BlockSpec((rb, cols), lambda *g, layer=layer: (layer * n_steps + step_index(*g), 0)))
        args.append(arr)
        for lo, hi in ranges:
            out_specs.append(pl.BlockSpec((rb, hi - lo), lambda *g: (step_index(*g), 0)))
            out_shape.append(jax.ShapeDtypeStruct((rows, hi - lo), BF16))
        rider_cols.append(ranges)
    return in_specs, args, out_specs, out_shape, tuple(rider_cols)


def _mm_kernel(*refs, act, has_bias, rider_cols, w_is_transposed):
    n_in = 3 if has_bias else 2
    n_riders = len(rider_cols)
    x_ref, w_ref = refs[0], refs[1]
    o_ref = refs[n_in + n_riders]
    acc = _nt(x_ref[...], w_ref[...]) if w_is_transposed else _dot(x_ref[...], w_ref[...])
    if has_bias:
        acc = acc + refs[2][...]
    if act == "gelu":
        acc = jax.nn.gelu(acc)
    o_ref[...] = acc.astype(o_ref.dtype)
    _round_riders(refs[n_in:n_in + n_riders], refs[n_in + n_riders + 1:], rider_cols)


def matmul(x, w, bias=None, act=None, out_dtype=BF16, riders=(), w_is_transposed=False):
    m, k = x.shape
    n = w.shape[0] if w_is_transposed else w.shape[1]
    tm, tn = min(MM_TM, m), min(MM_TN, n)
    assert m % tm == 0 and n % tn == 0
    ni, nj = m // tm, n // tn
    w_spec = pl.BlockSpec((tn, k), lambda i, j: (j, 0)) if w_is_transposed else pl.BlockSpec((k, tn), lambda i, j: (0, j))
    in_specs = [pl.BlockSpec((tm, k), lambda i, j: (i, 0)), w_spec]
    args = [x, w]
    if bias is not None:
        in_specs.append(pl.BlockSpec((1, tn), lambda i, j: (0, j)))
        args.append(bias.reshape(1, n).astype(F32))
    r_in, r_args, r_out, r_shape, rider_cols = _rider_specs(riders, ni * nj, lambda i, j: i * nj + j)
    outs = pl.pallas_call(
        functools.partial(_mm_kernel, act=act, has_bias=bias is not None, rider_cols=rider_cols,
                          w_is_transposed=w_is_transposed),
        grid=(ni, nj),
        in_specs=in_specs + r_in,
        out_specs=[pl.BlockSpec((tm, tn), lambda i, j: (i, j))] + r_out,
        out_shape=[jax.ShapeDtypeStruct((m, n), out_dtype)] + r_shape,
        compiler_params=_cparams(("parallel", "arbitrary")),
        name="matmul",
    )(*args, *r_args)
    return outs[0], list(outs[1:])


def _round_kernel(*refs, rider_cols):
    n = len(rider_cols)
    _round_riders(refs[:n], refs[n:], rider_cols)


def round_weights(riders, n_steps):
    r_in, r_args, r_out, r_shape, rider_cols = _rider_specs(riders, n_steps, lambda s: s)
    return pl.pallas_call(
        functools.partial(_round_kernel, rider_cols=rider_cols),
        grid=(n_steps,),
        in_specs=r_in,
        out_specs=r_out,
        out_shape=r_shape,
        compiler_params=_cparams(("parallel",)),
        name="round_weights",
    )(*r_args)


def _pack_rows(x):
    half = x.shape[1] // 2
    bits = lambda v: lax.bitcast_convert_type(v.astype(BF16).astype(F32), jnp.uint32)
    return bits(x[:, :half]) | (bits(x[:, half:]) >> 16)


def _unpack_rows(words):
    hi = lax.bitcast_convert_type(words & jnp.uint32(0xFFFF0000), F32)
    lo = lax.bitcast_convert_type(words << 16, F32)
    return hi, lo


def _layer_norm_rows(x, g, b):
    mu = jnp.mean(x, axis=-1, keepdims=True)
    xc = x - mu
    var = jnp.mean(xc * xc, axis=-1, keepdims=True)
    return xc * lax.rsqrt(var + LN_EPS) * g + b


def _ln_router_kernel(h_ref, mix_ref, g_ref, b_ref, rwt_ref, rb_ref,
                      h1_ref, h1p_ref, dest_ref, gate_ref, cnt_ref, carry_ref,
                      *, alpha, n_experts, cap):
    i = pl.program_id(0)
    epg = n_experts // N_GROUPS
    tm = h_ref.shape[0]

    @pl.when(i == 0)
    def _():
        carry_ref[...] = jnp.zeros_like(carry_ref)

    x = alpha * h_ref[...] + mix_ref[...].astype(F32)
    y = _layer_norm_rows(x, g_ref[...], b_ref[...])
    h1_ref[...] = y
    h1p_ref[...] = _pack_rows(y)

    logits = _nt_split(rwt_ref[...], y)
    mx = jnp.max(logits, axis=0, keepdims=True)
    ex = jnp.exp(logits - mx)
    probs = ex / jnp.sum(ex, axis=0, keepdims=True)
    sel = probs + rb_ref[...]

    srow = [sel[e:e + 1, :] for e in range(n_experts)]
    prow = [probs[e:e + 1, :] for e in range(n_experts)]

    def top2_sum(vals):
        best = None
        for a in range(len(vals)):
            for c in range(a + 1, len(vals)):
                s = vals[a] + vals[c]
                best = s if best is None else jnp.maximum(best, s)
        return best

    gscore = [top2_sum(srow[g * epg:(g + 1) * epg]) for g in range(N_GROUPS)]
    best_g = jnp.zeros_like(gscore[0], dtype=jnp.int32)
    best_s = gscore[0]
    for g in range(1, N_GROUPS):
        upd = gscore[g] > best_s
        best_g = jnp.where(upd, g, best_g)
        best_s = jnp.where(upd, gscore[g], best_s)

    def pick(rows, j):
        out = rows[j]
        for g in range(1, N_GROUPS):
            out = jnp.where(best_g == g, rows[g * epg + j], out)
        return out

    in_s = [pick(srow, j) for j in range(epg)]
    in_p = [pick(prow, j) for j in range(epg)]

    def argmax_first(vals, exclude=None):
        bi = None
        bv = None
        for j, v in enumerate(vals):
            if exclude is not None:
                v = jnp.where(exclude == j, -jnp.inf, v)
            if bi is None:
                bi = jnp.zeros_like(best_g)
                bv = v
            else:
                upd = v > bv
                bi = jnp.where(upd, j, bi)
                bv = jnp.where(upd, v, bv)
        return bi

    loc0 = argmax_first(in_s)
    loc1 = argmax_first(in_s, exclude=loc0)

    def take(vals, idx):
        out = vals[0]
        for j in range(1, len(vals)):
            out = jnp.where(idx == j, vals[j], out)
        return out

    p0 = take(in_p, loc0)
    p1 = take(in_p, loc1)
    psum = p0 + p1
    gate_ref[...] = jnp.concatenate([p0 / psum, p1 / psum], axis=0)[None]

    idx0 = best_g * epg + loc0
    idx1 = best_g * epg + loc1

    eid = lax.broadcasted_iota(jnp.int32, (n_experts, tm), 0)
    hit0 = eid == idx0
    hit1 = eid == idx1
    member = jnp.where(hit0, 1.0, jnp.where(hit1, 1.0, 0.0))
    srci = lax.broadcasted_iota(jnp.int32, (tm, tm), 0)
    dsti = lax.broadcasted_iota(jnp.int32, (tm, tm), 1)
    upper = jnp.where(srci < dsti, 1.0, 0.0).astype(BF16)
    prefix = _dot(member.astype(BF16), upper)
    carry = carry_ref[...]
    pos = eid.astype(F32) * float(cap) + carry[:, :1] + prefix
    dest0 = jnp.sum(jnp.where(hit0, pos, 0.0), axis=0, keepdims=True)
    dest1 = jnp.sum(jnp.where(hit1, pos, 0.0), axis=0, keepdims=True)
    dest_ref[...] = jnp.concatenate([dest0, dest1], axis=0).astype(jnp.int32)[None]
    carry = carry + jnp.sum(member, axis=1, keepdims=True)
    carry_ref[...] = carry
    cnt_ref[...] = carry.astype(jnp.int32)


def ln_router(h, mix, ln_g, ln_b, router_wt, router_bias, *, alpha, cap):
    n, d = h.shape
    n_experts = router_wt.shape[0]
    tm = min(TOK_TILE, n)
    nb = n // tm
    row = lambda i: (i, 0)
    const = lambda i: (0, 0)
    return pl.pallas_call(
        functools.partial(_ln_router_kernel, alpha=alpha, n_experts=n_experts, cap=cap),
        grid=(nb,),
        in_specs=[pl.BlockSpec((tm, d), row), pl.BlockSpec((tm, d), row),
                  pl.BlockSpec((1, d), const), pl.BlockSpec((1, d), const),
                  pl.BlockSpec((n_experts, d), const), pl.BlockSpec((n_experts, 1), const)],
        out_specs=[pl.BlockSpec((tm, d), row), pl.BlockSpec((tm, d // 2), row),
                   pl.BlockSpec((1, TOP_K, tm), lambda i: (i, 0, 0)),
                   pl.BlockSpec((1, TOP_K, tm), lambda i: (i, 0, 0)),
                   pl.BlockSpec((n_experts, V7X_LANES), const)],
        out_shape=[jax.ShapeDtypeStruct((n, d), F32), jax.ShapeDtypeStruct((n, d // 2), jnp.uint32),
                   jax.ShapeDtypeStruct((nb, TOP_K, tm), jnp.int32),
                   jax.ShapeDtypeStruct((nb, TOP_K, tm), F32),
                   jax.ShapeDtypeStruct((n_experts, V7X_LANES), jnp.int32)],
        scratch_shapes=[pltpu.VMEM((n_experts, V7X_LANES), F32)],
        compiler_params=_cparams(("arbitrary",)),
        name="ln_router",
    )(h, mix, ln_g.reshape(1, d), ln_b.reshape(1, d), router_wt, router_bias.reshape(n_experts, 1))


def _row_copy(src_ref, src_row, dst_ref, dst_row, sem):
    return pltpu.make_async_copy(src_ref.at[pl.ds(src_row, 1), :], dst_ref.at[pl.ds(dst_row, 1), :], sem)


def _dispatch_kernel(cnt_ref, dest_ref, x_ref, xs_ref, zero_ref, sem, zsem, *, cap, n_experts):
    i = pl.program_id(0)
    tm = x_ref.shape[0]

    for r in range(tm):
        for k in range(TOP_K):
            _row_copy(x_ref, r, xs_ref, dest_ref[k, r], sem).start(priority=k)

    @pl.when(i == pl.num_programs(0) - 1)
    def _():
        zero_ref[...] = jnp.zeros_like(zero_ref)

        def pad_rows(e, do):
            cnt = cnt_ref[e]
            n_pad = lax.rem(EXP_TILE - lax.rem(cnt, EXP_TILE), EXP_TILE)

            def one(r, c):
                do(_row_copy(zero_ref, 0, xs_ref, e * cap + cnt + r, zsem))
                return c

            lax.fori_loop(0, n_pad, one, 0)
            return do

        lax.fori_loop(0, n_experts, lambda e, c: (pad_rows(e, lambda cp: cp.start()), c)[1], 0)
        lax.fori_loop(0, n_experts, lambda e, c: (pad_rows(e, lambda cp: cp.wait()), c)[1], 0)

    def wait(r, c):
        for k in range(TOP_K):
            _row_copy(x_ref, 0, xs_ref, 0, sem).wait()
        return c

    lax.fori_loop(0, tm, wait, 0, unroll=WAIT_UNROLL)


def dispatch(x, dest, counts, *, cap):
    n, d = x.shape
    n_experts = counts.shape[0]
    nb, _, tm = dest.shape
    return pl.pallas_call(
        functools.partial(_dispatch_kernel, cap=cap, n_experts=n_experts),
        grid_spec=pltpu.PrefetchScalarGridSpec(
            num_scalar_prefetch=1,
            grid=(nb,),
            in_specs=[pl.BlockSpec((None, TOP_K, tm), lambda i, c: (i, 0, 0), memory_space=pltpu.SMEM),
                      pl.BlockSpec((tm, d), lambda i, c: (i, 0))],
            out_specs=pl.BlockSpec(memory_space=pl.ANY),
            scratch_shapes=[pltpu.VMEM((V7X_SUBLANES, d), x.dtype),
                            pltpu.SemaphoreType.DMA(()), pltpu.SemaphoreType.DMA(())],
        ),
        out_shape=jax.ShapeDtypeStruct((n_experts * cap, d), x.dtype),
        compiler_params=_cparams(("arbitrary",)),
        name="dispatch",
    )(counts, dest, x)


def _expert_kernel(blk_ref, exp_ref, valid_ref, x_ref, wg_ref, wu_ref, wd_ref, o_ref):
    s = pl.program_id(0)

    @pl.when(valid_ref[s] == 1)
    def _():
        hi, lo = _unpack_rows(x_ref[...])
        x = jnp.concatenate([hi.astype(BF16), lo.astype(BF16)], axis=1)
        gate = _dot(x, wg_ref[...])
        up = _dot(x, wu_ref[...])
        act = (jax.nn.silu(gate) * up).astype(BF16)
        o_ref[...] = _pack_rows(_dot(act, wd_ref[...]))


def expert_ffn(xs, w_gate, w_up, w_down, counts, *, cap, n_tokens):
    d = xs.shape[1]
    n_experts, f = w_gate.shape[0], w_gate.shape[2]
    tm = EXP_TILE
    n_tiles = (TOP_K * n_tokens) // tm + n_experts
    tiles_e = (counts + tm - 1) // tm
    cum = jnp.cumsum(tiles_e)
    total = cum[-1]
    slot = jnp.minimum(jnp.arange(n_tiles, dtype=jnp.int32), total - 1)
    e_of = jnp.sum((slot[:, None] >= cum[None, :]).astype(jnp.int32), axis=1)
    first = jnp.take(cum - tiles_e, e_of)
    blk = (e_of * (cap // tm) + slot - first).astype(jnp.int32)
    valid = (jnp.arange(n_tiles) < total).astype(jnp.int32)
    xmap = lambda s, blk, ex, va: (blk[s], 0)
    return pl.pallas_call(
        _expert_kernel,
        grid_spec=pltpu.PrefetchScalarGridSpec(
            num_scalar_prefetch=3,
            grid=(n_tiles,),
            in_specs=[pl.BlockSpec((tm, d), xmap),
                      pl.BlockSpec((None, 2 * d, f), lambda s, blk, ex, va: (ex[s], 0, 0)),
                      pl.BlockSpec((None, 2 * d, f), lambda s, blk, ex, va: (ex[s], 0, 0)),
                      pl.BlockSpec((None, f, 2 * d), lambda s, blk, ex, va: (ex[s], 0, 0))],
            out_specs=pl.BlockSpec((tm, d), xmap),
        ),
        out_shape=jax.ShapeDtypeStruct(xs.shape, jnp.uint32),
        compiler_params=_cparams(("arbitrary",)),
        name="expert_ffn",
    )(blk, e_of.astype(jnp.int32), valid, xs, w_gate, w_up, w_down)


def _combine_kernel(dest_ref, next_dest_ref, h_ref, gate_ref, g_ref, b_ref, ys_ref, o_ref, ob_ref, buf_ref, sems,
                    *, alpha):
    i = pl.program_id(0)
    last = pl.num_programs(0) - 1
    tm = h_ref.shape[0]
    parity = lax.rem(i, 2)

    def gather(idx_ref, to_slot):
        for r in range(tm):
            for k in range(TOP_K):
                _row_copy(ys_ref, idx_ref[k, r], buf_ref.at[to_slot, k], r, sems.at[to_slot]).start(priority=k)

    def wait_slot(slot):
        def wait(r, c):
            for k in range(TOP_K):
                _row_copy(ys_ref, 0, buf_ref.at[slot, k], 0, sems.at[slot]).wait()
            return c

        lax.fori_loop(0, tm, wait, 0, unroll=WAIT_UNROLL)

    @pl.when(i == 0)
    def _():
        gather(dest_ref, 0)

    def step(cur):
        wait_slot(cur)
        gather(next_dest_ref, 1 - cur)
        gates = gate_ref[...]
        hi0, lo0 = _unpack_rows(buf_ref[cur, 0])
        hi1, lo1 = _unpack_rows(buf_ref[cur, 1])
        ffn = jnp.concatenate([gates[:, 0:1] * hi0 + gates[:, 1:2] * hi1,
                               gates[:, 0:1] * lo0 + gates[:, 1:2] * lo1], axis=1)
        x = alpha * h_ref[...] + ffn
        y = _layer_norm_rows(x, g_ref[...], b_ref[...])
        o_ref[...] = y
        ob_ref[...] = y.astype(BF16)

        @pl.when(i == last)
        def _():
            wait_slot(1 - cur)

    for cur in range(2):
        pl.when(parity == cur)(functools.partial(step, cur))


def combine(h, dest, gates_cols, ln_g, ln_b, ys, *, alpha):
    n, d = h.shape
    nb, _, tm = dest.shape
    row = lambda i: (i, 0)
    const = lambda i: (0, 0)
    return pl.pallas_call(
        functools.partial(_combine_kernel, alpha=alpha),
        grid=(nb,),
        in_specs=[pl.BlockSpec((None, TOP_K, tm), lambda i: (i, 0, 0), memory_space=pltpu.SMEM),
                  pl.BlockSpec((None, TOP_K, tm), lambda i: (jnp.minimum(i + 1, nb - 1), 0, 0),
                               memory_space=pltpu.SMEM),
                  pl.BlockSpec((tm, d), row), pl.BlockSpec((tm, TOP_K), row),
                  pl.BlockSpec((1, d), const), pl.BlockSpec((1, d), const),
                  pl.BlockSpec(memory_space=pl.ANY)],
        out_specs=[pl.BlockSpec((tm, d), row), pl.BlockSpec((tm, d), row)],
        out_shape=[jax.ShapeDtypeStruct((n, d), F32), jax.ShapeDtypeStruct((n, d), BF16)],
        scratch_shapes=[pltpu.VMEM((2, TOP_K, tm, d // 2), jnp.uint32), pltpu.SemaphoreType.DMA((2,))],
        compiler_params=_cparams(("arbitrary",)),
        name="combine",
    )(dest, dest, h, gates_cols, ln_g.reshape(1, d), ln_b.reshape(1, d), ys)


def moe_block(h, mix, ln_g, ln_b, router_wt, router_bias, w_gate, w_up, w_down, *, alpha):
    n, d = h.shape
    cap = n + EXP_TILE
    h1, h1p, dest, gates, cnt = ln_router(h, mix, ln_g[0], ln_b[0], router_wt, router_bias,
                                          alpha=alpha, cap=cap)
    counts = cnt[:, 0]
    xs = dispatch(h1p, dest, counts, cap=cap)
    ys = expert_ffn(xs, w_gate, w_up, w_down, counts, cap=cap, n_tokens=n)
    gates_cols = jnp.transpose(gates, (0, 2, 1)).reshape(n, TOP_K)
    return combine(h1, dest, gates_cols, ln_g[1], ln_b[1], ys, alpha=alpha)


def _chunk_masks(n, chunk):
    ri = lax.broadcasted_iota(jnp.int32, (n, n), 0)
    ci = lax.broadcasted_iota(jnp.int32, (n, n), 1)
    same = (ri // chunk) == (ci // chunk)
    return same & (ri >= ci), same & (ri > ci)


def _gate_params_kernel(h_ref, w_ref, alog_ref, dtb_ref, o_ref, *, n_heads):
    tm = h_ref.shape[0]
    h_hi, h_lo = _split_bf16(h_ref[...])
    w_hi, w_lo = _split_bf16(w_ref[...])
    ab = _nt(h_hi, w_hi) + (_nt(h_hi, w_lo) + _nt(h_lo, w_hi))
    g = -jnp.exp(alog_ref[...]) * jax.nn.softplus(ab + dtb_ref[...])
    beta = jax.nn.sigmoid(ab)
    causal, _ = _chunk_masks(tm, DN_CHUNK)
    tri = jnp.where(causal, 1.0, 0.0).astype(BF16)
    g_hi, g_rest = _split_bf16(g)
    g_mid, g_lo = _split_bf16(g - g_hi.astype(F32))
    del g_rest
    gc = _dot(tri, g_hi) + (_dot(tri, g_mid) + _dot(tri, g_lo))
    lane = lax.broadcasted_iota(jnp.int32, ab.shape, 1)
    o_ref[...] = jnp.where(lane < n_heads, gc, beta)


def gate_params(h, w_in_t_all, layer, col0, a_log, dt_bias):
    n, d = h.shape
    n_heads = a_log.shape[0]
    assert col0 % V7X_LANES == 0 and 2 * n_heads <= V7X_LANES
    tm = min(AB_TILE, n)
    lanes = lambda v: jnp.pad(v.astype(F32), (0, V7X_LANES - n_heads)).reshape(1, V7X_LANES)
    const = lambda i: (0, 0)
    return pl.pallas_call(
        functools.partial(_gate_params_kernel, n_heads=n_heads),
        grid=(n // tm,),
        in_specs=[pl.BlockSpec((tm, d), lambda i: (i, 0)),
                  pl.BlockSpec((None, V7X_LANES, d), lambda i: (layer, col0 // V7X_LANES, 0)),
                  pl.BlockSpec((1, V7X_LANES), const), pl.BlockSpec((1, V7X_LANES), const)],
        out_specs=pl.BlockSpec((tm, V7X_LANES), lambda i: (i, 0)),
        out_shape=jax.ShapeDtypeStruct((n, V7X_LANES), F32),
        compiler_params=_cparams(("parallel",)),
        name="gate_params",
    )(h, w_in_t_all, lanes(a_log), lanes(dt_bias))


def _unit_lower_inverses(mats):
    n = mats[0].shape[0]
    ri = lax.broadcasted_iota(jnp.int32, (n, n), 0)
    ci = lax.broadcasted_iota(jnp.int32, (n, n), 1)
    differ = ri ^ ci
    eye = jnp.where(ri == ci, 1.0, 0.0)
    base = (differ >> 1) == 0
    invs = [eye - jnp.where(base, a, 0.0) for a in mats]
    level = 1
    while (1 << level) < DN_CHUNK:
        at_level = (differ >> level) == 1
        invbs = [inv.astype(BF16) for inv in invs]
        halves = [_dot(jnp.where(at_level, a, 0.0).astype(BF16), invb).astype(BF16)
                  for a, invb in zip(mats, invbs)]
        invs = [inv - _dot(invb, half) for inv, invb, half in zip(invs, invbs, halves)]
        level += 1
    return invs


def _deltanet_kernel(q_ref, k_ref, v_ref, z_ref, cq_ref, ck_ref, cv_ref, gb_ref, gt_ref, ng_ref, y_any,
                     o_ref, s_ref, carry_ref, *, hb, dk):
    del y_any
    t = pl.program_id(2)
    tb = q_ref.shape[0]
    scale = dk ** -0.5

    @pl.when(t == 0)
    def _():
        s_ref[...] = jnp.zeros_like(s_ref)
        carry_ref[...] = jnp.zeros_like(carry_ref)

    def conv_silu(x_ref, w_ref, slot):
        raw = x_ref[...].astype(F32)
        ext = jnp.concatenate([carry_ref[slot], raw], axis=0)
        w = w_ref[...]
        taps = w.shape[0]
        acc = None
        for j in range(taps):
            lag = taps - 1 - j
            window = pltpu.roll(ext, lag, axis=0)[V7X_SUBLANES:] if lag else raw
            term = window * w[j:j + 1, :]
            acc = term if acc is None else acc + term
        carry_ref[slot] = raw[tb - V7X_SUBLANES:tb]
        return jax.nn.silu(acc)

    qa = conv_silu(q_ref, cq_ref, 0)
    ka = conv_silu(k_ref, ck_ref, 1)
    va = conv_silu(v_ref, cv_ref, 2)
    za = z_ref[...].astype(F32)
    gbl = gb_ref[...]
    gt = gt_ref[...]
    sr = min(DN_SOLVE, tb)
    n_sub = tb // sr
    causal, strict = _chunk_masks(sr, DN_CHUNK)

    heads = range(hb)
    n_chunks = tb // DN_CHUNK
    qn, kn, gc_col, a, rhs, qk = [], [], [], [], [], []
    for hh in heads:
        sl = slice(hh * dk, (hh + 1) * dk)
        q, k, v = qa[:, sl], ka[:, sl], va[:, sl]
        qn.append(q * lax.rsqrt(jnp.sum(q * q, axis=-1, keepdims=True) + RMS_EPS) * scale)
        kn.append(k * lax.rsqrt(jnp.sum(k * k, axis=-1, keepdims=True) + RMS_EPS))
        gc_col.append(gbl[:, hh:hh + 1])
        beta = gbl[:, hb + hh:hb + hh + 1]
        kb = kn[hh] * beta
        rhs.append(jnp.concatenate([v * beta, kb * jnp.exp(gc_col[hh])], axis=1).astype(BF16))
        for p in range(n_sub):
            rows = slice(p * sr, (p + 1) * sr)
            gc_row = gt[hh:hh + 1, rows]
            decay = jnp.where(causal, jnp.exp(jnp.where(causal, gc_col[hh][rows] - gc_row, 0.0)), 0.0)
            knb = kn[hh][rows].astype(BF16)
            a.append(jnp.where(strict, _nt(kb[rows].astype(BF16), knb) * decay, 0.0))
            qk.append(jnp.where(causal, _nt(qn[hh][rows].astype(BF16), knb) * decay, 0.0).astype(BF16))
    tinv = _unit_lower_inverses(a)
    solb = [jnp.concatenate([_dot(tinv[hh * n_sub + p].astype(BF16), rhs[hh][p * sr:(p + 1) * sr]).astype(BF16)
                             for p in range(n_sub)], axis=0) for hh in heads]
    qk_sol = [jnp.concatenate([_dot(qk[hh * n_sub + p], solb[hh][p * sr:(p + 1) * sr]) for p in range(n_sub)],
                              axis=0) for hh in heads]
    q_eff = [(qn[hh] * jnp.exp(gc_col[hh]) - qk_sol[hh][:, dk:]).astype(BF16) for hh in heads]
    g_last = [[gc_col[hh][(c + 1) * DN_CHUNK - 1:(c + 1) * DN_CHUNK, :] for c in range(n_chunks)] for hh in heads]
    kd_sol = [[_tn((kn[hh][c * DN_CHUNK:(c + 1) * DN_CHUNK]
                    * jnp.exp(g_last[hh][c] - gc_col[hh][c * DN_CHUNK:(c + 1) * DN_CHUNK])).astype(BF16),
                   solb[hh][c * DN_CHUNK:(c + 1) * DN_CHUNK])
               for c in range(n_chunks)] for hh in heads]
    s = [s_ref[hh] for hh in heads]
    outs = [[] for _ in heads]
    for c in range(n_chunks):
        lo, hi = c * DN_CHUNK, (c + 1) * DN_CHUNK
        for hh in heads:
            sb = s[hh].astype(BF16)
            outs[hh].append(qk_sol[hh][lo:hi, :dk] + _dot(q_eff[hh][lo:hi], sb))
            s[hh] = (s[hh] * jnp.exp(g_last[hh][c]) - _dot(kd_sol[hh][c][:, dk:].astype(BF16), sb)
                     + kd_sol[hh][c][:, :dk])
    for hh in heads:
        s_ref[hh] = s[hh]
        o = jnp.concatenate(outs[hh], axis=0)
        o = o * lax.rsqrt(jnp.mean(o * o, axis=-1, keepdims=True) + RMS_EPS) * ng_ref[...]
        o = o * jax.nn.silu(za[:, hh * dk:(hh + 1) * dk])
        o_ref[:, hh * dk:(hh + 1) * dk] = o.astype(o_ref.dtype)


def deltanet(qkvz, conv_w, gparams, norm_g, y, *, batch, n_heads):
    n = qkvz.shape[0]
    dk = norm_g.shape[0]
    hb = min(DN_HB, n_heads)
    w = hb * dk
    nhb = n_heads // hb
    seq = n // batch
    tb = min(DN_TB, seq)
    nt = seq // tb
    gc = gparams[:, :n_heads].reshape(n, nhb, hb)
    beta = gparams[:, n_heads:2 * n_heads].reshape(n, nhb, hb)
    cols = jnp.concatenate([gc, beta], axis=-1).transpose(1, 0, 2)
    cols = jnp.pad(cols, ((0, 0), (0, 0), (0, V7X_LANES - 2 * hb)))
    rows = jnp.pad(gc.transpose(1, 2, 0), ((0, 0), (0, V7X_SUBLANES - hb), (0, 0)))
    rmap = lambda part: (lambda b, h, t: (b * nt + t, part * nhb + h))
    cmap = lambda part: (lambda b, h, t: (0, part * nhb + h))
    taps = conv_w.shape[0]
    return pl.pallas_call(
        functools.partial(_deltanet_kernel, hb=hb, dk=dk),
        grid=(batch, nhb, nt),
        in_specs=[pl.BlockSpec((tb, w), rmap(0)), pl.BlockSpec((tb, w), rmap(1)),
                  pl.BlockSpec((tb, w), rmap(2)), pl.BlockSpec((tb, w), rmap(3)),
                  pl.BlockSpec((taps, w), cmap(0)), pl.BlockSpec((taps, w), cmap(1)),
                  pl.BlockSpec((taps, w), cmap(2)),
                  pl.BlockSpec((None, tb, V7X_LANES), lambda b, h, t: (h, b * nt + t, 0)),
                  pl.BlockSpec((None, V7X_SUBLANES, tb), lambda b, h, t: (h, 0, b * nt + t)),
                  pl.BlockSpec((1, dk), lambda b, h, t: (0, 0)),
                  pl.BlockSpec(memory_space=pl.ANY)],
        out_specs=pl.BlockSpec((tb, w), rmap(0)),
        out_shape=jax.ShapeDtypeStruct(y.shape, y.dtype),
        scratch_shapes=[pltpu.VMEM((hb, dk, dk), F32), pltpu.VMEM((3, V7X_SUBLANES, w), F32)],
        input_output_aliases={10: 0},
        compiler_params=_cparams(("parallel", "parallel", "arbitrary")),
        name="deltanet",
    )(qkvz, qkvz, qkvz, qkvz, conv_w, conv_w, conv_w, cols, rows, norm_g.reshape(1, dk), y)


POOL_HALO = 16


def _pool_kernel(x_ref, w_ref, b_ref, sc_ref, o_ref, carry_ref):
    t = pl.program_id(1)
    tt = x_ref.shape[0]
    cg = w_ref.shape[1]

    @pl.when(t == 0)
    def _():
        carry_ref[...] = jnp.zeros_like(carry_ref)

    x = x_ref[...].astype(F32)
    ext = jnp.concatenate([carry_ref[...], x], axis=0)
    carry_ref[...] = x[tt - POOL_HALO:tt]
    pos = t * tt + lax.broadcasted_iota(jnp.int32, (tt, 1), 0)
    for gi, win in enumerate(POOL_WINDOWS):
        cols = slice(gi * cg, (gi + 1) * cg)
        cur = ext[:, cols]
        span = 1
        while span < win:
            cur = cur[span:] + cur[:cur.shape[0] - span]
            span *= 2
        lo = POOL_HALO - (win - 1)
        total = cur[lo:lo + tt]
        count = jnp.minimum(pos + 1, win).astype(F32)
        pooled = total / count - x[:, cols]
        mixed = _dot(pooled.astype(BF16), w_ref[gi]) + b_ref[:, cols]
        o_ref[:, cols] = (mixed * sc_ref[:, cols]).astype(o_ref.dtype)


def pool(xp, pool_w, pool_b, pool_scale, *, batch, out_width):
    n, p = xp.shape
    assert out_width % p == 0 and max(POOL_WINDOWS) <= POOL_HALO
    seq = n // batch
    tt = min(POOL_TILE, seq)
    nt = seq // tt
    g, cg, _ = pool_w.shape
    const = lambda b, t: (0, 0)
    return pl.pallas_call(
        _pool_kernel,
        grid=(batch, nt),
        in_specs=[pl.BlockSpec((tt, p), lambda b, t: (b * nt + t, 0)),
                  pl.BlockSpec((g, cg, cg), lambda b, t: (0, 0, 0)),
                  pl.BlockSpec((1, p), const), pl.BlockSpec((1, p), const)],
        out_specs=pl.BlockSpec((tt, p), lambda b, t: (b * nt + t, out_width // p - 1)),
        out_shape=jax.ShapeDtypeStruct((n, out_width), BF16),
        scratch_shapes=[pltpu.VMEM((POOL_HALO, p), F32)],
        compiler_params=_cparams(("parallel", "arbitrary")),
        name="pool",
    )(xp, pool_w, pool_b.reshape(1, p).astype(F32), pool_scale.reshape(1, p).astype(F32))


def _sgu_kernel(u_ref, v_ref, g_ref, b_ref, ws_ref, bs_ref, o_ref):
    tt, width = u_ref.shape
    n_heads, chunk, _ = ws_ref.shape
    hd = width // n_heads
    vn = _layer_norm_rows(v_ref[...].astype(F32), g_ref[...], b_ref[...]).astype(BF16)
    ri = lax.broadcasted_iota(jnp.int32, (chunk, chunk), 0)
    ci = lax.broadcasted_iota(jnp.int32, (chunk, chunk), 1)
    for g in range(n_heads):
        cols = slice(g * hd, (g + 1) * hd)
        wsg = jnp.where(ri >= ci, ws_ref[g], 0.0).astype(BF16)
        bias = bs_ref[:, g:g + 1]
        for c in range(tt // chunk):
            rows = slice(c * chunk, (c + 1) * chunk)
            mixed = _dot(wsg, vn[rows, cols]) + bias
            o_ref[rows, cols] = (u_ref[rows, cols].astype(F32) * mixed).astype(o_ref.dtype)


def sgu(uv, ln_g, ln_b, w_s, b_s):
    n, two_w = uv.shape
    width = two_w // 2
    n_heads, chunk, _ = w_s.shape
    tt = min(SGU_TILE, n)
    assert tt % chunk == 0
    const = lambda i: (0, 0)
    return pl.pallas_call(
        _sgu_kernel,
        grid=(n // tt,),
        in_specs=[pl.BlockSpec((tt, width), lambda i: (i, 0)), pl.BlockSpec((tt, width), lambda i: (i, 1)),
                  pl.BlockSpec((1, width), const), pl.BlockSpec((1, width), const),
                  pl.BlockSpec((n_heads, chunk, chunk), lambda i: (0, 0, 0)),
                  pl.BlockSpec((chunk, n_heads), const)],
        out_specs=pl.BlockSpec((tt, width), lambda i: (i, 0)),
        out_shape=jax.ShapeDtypeStruct((n, width), BF16),
        compiler_params=_cparams(("parallel",)),
        name="sgu",
    )(uv, uv, ln_g.reshape(1, width), ln_b.reshape(1, width), w_s, jnp.transpose(b_s))


def even_mixer(hf, hb16, w_bf16, w_in_t_all, i, w_conv, a_log, dt_bias, norm_g, pool_w, pool_b, pool_scale,
               riders_in, riders_out, *, batch):
    w_qkvz_t, w_xp_t, w_out = w_bf16
    n_heads = a_log.shape[0]
    qkvz, rounded_in = matmul(hb16, w_qkvz_t, riders=riders_in, w_is_transposed=True)
    xp, _ = matmul(hb16, w_xp_t, w_is_transposed=True)
    gparams = gate_params(hf, w_in_t_all, i, w_qkvz_t.shape[0], a_log, dt_bias)
    y = pool(xp, pool_w.astype(BF16), pool_b, pool_scale, batch=batch, out_width=w_out.shape[0])
    y = deltanet(qkvz, w_conv, gparams, norm_g, y, batch=batch, n_heads=n_heads)
    mix, rounded_out = matmul(y, w_out, riders=riders_out)
    return mix, rounded_in, rounded_out


def odd_mixer(hb16, w_bf16, b_in, ln_g, ln_b, w_s, b_s, riders_in, riders_out):
    w_in, w_out = w_bf16
    uv, rounded_in = matmul(hb16, w_in, bias=b_in, act="gelu", riders=riders_in)
    gated = sgu(uv, ln_g, ln_b, w_s, b_s)
    mix, rounded_out = matmul(gated, w_out, riders=riders_out)
    return mix, rounded_in, rounded_out


def kernel(x, ev_w_in, ev_conv, ev_a_log, ev_dt_bias, ev_norm_g, ev_pool_w, ev_pool_b, ev_pool_scale, ev_w_out, od_w_in, od_b_in, od_ln_g, od_ln_b, od_w_s, od_b_s, od_w_out, ln_g, ln_b, router_w, router_bias, moe_w_gate, moe_w_up, moe_w_down):
    batch, seq, d = x.shape
    depth = ln_g.shape[0]
    alpha = (2.0 * depth) ** 0.25
    hf = x.reshape(batch * seq, d)
    router_wt = jnp.transpose(router_w)
    _, n_experts, _, f = moe_w_gate.shape
    wg_rows = moe_w_gate.reshape(depth * n_experts * d, f)
    wu_rows = moe_w_up.reshape(depth * n_experts * d, f)
    wd_rows = moe_w_down.reshape(depth * n_experts * f, d)
    o0 = 4 * ev_a_log.shape[1] * ev_norm_g.shape[1]
    o1 = o0 + 2 * ev_a_log.shape[1]
    ev_w_in_t = jnp.swapaxes(ev_w_in, 1, 2)
    stack = lambda w: w.reshape(-1, w.shape[-1])

    def mixer_riders(layer):
        j = layer // 2
        if layer % 2 == 0:
            return ((ev_w_in_t, None, j, ((0, o0), (o1, ev_w_in_t.shape[1]))),
                    (stack(ev_w_out), ev_w_out.shape[1], j, None))
        return ((stack(od_w_in), od_w_in.shape[1], j, None), (stack(od_w_out), od_w_out.shape[1], j, None))

    n_round = max(1, min(ev_w_out.shape[1], ev_w_in_t.shape[1] - o1) // ROUND_ROWS)
    *w_bf16, hb16 = round_weights(mixer_riders(0) + ((hf, hf.shape[0], 0, None),), n_round)
    for layer in range(depth):
        i = layer // 2
        riders = ((wg_rows, n_experts * d, layer, None), (wu_rows, n_experts * d, layer, None),
                  (wd_rows, n_experts * f, layer, None))
        next_riders = mixer_riders(layer + 1) if layer + 1 < depth else ()
        if layer % 2 == 0:
            mix, (wg, wu, wd), rounded = even_mixer(
                hf, hb16, w_bf16, ev_w_in_t, i, ev_conv[i], ev_a_log[i], ev_dt_bias[i], ev_norm_g[i],
                ev_pool_w[i], ev_pool_b[i], ev_pool_scale[i], riders, next_riders, batch=batch)
        else:
            mix, (wg, wu, wd), rounded = odd_mixer(hb16, w_bf16, od_b_in[i], od_ln_g[i], od_ln_b[i], od_w_s[i],
                                                   od_b_s[i], riders, next_riders)
        w_bf16 = rounded
        hf, hb16 = moe_block(hf, mix, ln_g[layer], ln_b[layer], router_wt, router_bias,
                             wg.reshape(n_experts, d, f), wu.reshape(n_experts, d, f),
                             wd.reshape(n_experts, f, d), alpha=alpha)
    return hf.reshape(batch, seq, d)
```
